```python
import math
import jax, jax.numpy as jnp
from jax import lax
import numpy as np


D_MODEL = 2048
BATCH = 16
SEQ = 2048
DEPTH = 2
DEC_BATCH = 32
DEC_SEQ = 64
PAST_LEN = 2048

CHUNK = 64
Q_BLOCK = 128
EPS = 1e-6
NEG_INF = -1e30
POOL_WIDTH = 1024
POOL_WINDOWS = (2, 4, 8, 16)
POOL_GROUPS = len(POOL_WINDOWS)
POOL_GROUP = POOL_WIDTH // POOL_GROUPS
POOL_OUT_GROUP = D_MODEL // POOL_GROUPS
POOL_HIST = max(POOL_WINDOWS) - 1
CONV_WIDTH = 1024
CONV_K = 31
CONV_HIST = CONV_K - 1
N_HEADS = 16
Q_LORA = 512
KV_LORA = 512
QK_NOPE = 128
QK_ROPE = 64
V_DIM = 128
ROPE_THETA = 10000.0
ATTN_SCALE = 1.0 / math.sqrt(QK_NOPE + QK_ROPE)
D_FF = 5632
FFN_K = 3
FFN_HIST = FFN_K - 1
N_BRANCH = 3

OFF_POOL = 0
OFF_GLU = OFF_POOL + POOL_WIDTH
OFF_Q = OFF_GLU + 2 * CONV_WIDTH
OFF_KV = OFF_Q + Q_LORA
OFF_KR = OFF_KV + KV_LORA
OFF_GATE = OFF_KR + QK_ROPE
N_IN = OFF_GATE + N_BRANCH * D_MODEL

kernel_name = 'hybrid_pool_conformer_mla_stream_step'


def rms_norm(x, g):
    xf = x.astype(jnp.float32)
    y = xf * lax.rsqrt(jnp.mean(xf * xf, axis=-1, keepdims=True) + EPS)
    return (y * g.astype(jnp.float32)).astype(x.dtype)


def layer_norm(x, g, b):
    xf = x.astype(jnp.float32)
    mu = jnp.mean(xf, axis=-1, keepdims=True)
    var = jnp.mean(jnp.square(xf - mu), axis=-1, keepdims=True)
    y = (xf - mu) * lax.rsqrt(var + EPS)
    return (y * g.astype(jnp.float32) + b.astype(jnp.float32)).astype(x.dtype)


def apply_rope(x, pos):
    half = x.shape[-1] // 2
    inv = ROPE_THETA ** (-jnp.arange(half, dtype=jnp.float32) / half)
    ang = pos.astype(jnp.float32)[:, None] * inv[None, :]
    shape = (1, pos.shape[0]) + (1,) * (x.ndim - 3) + (half,)
    cos = jnp.cos(ang).reshape(shape)
    sin = jnp.sin(ang).reshape(shape)
    xf = x.astype(jnp.float32)
    x1, x2 = xf[..., :half], xf[..., half:]
    return jnp.concatenate([x1 * cos - x2 * sin, x2 * cos + x1 * sin], axis=-1).astype(x.dtype)


def depthwise_causal(ext, w, b):
    y = lax.conv_general_dilated(ext, w[:, None, :].astype(ext.dtype), window_strides=(1,),
                                 padding='VALID', dimension_numbers=('NWC', 'WIO', 'NWC'),
                                 feature_group_count=ext.shape[-1])
    return y + b.astype(ext.dtype)


def pool_mixer(u, hist, pos, w_pool, pool_scale):
    T = u.shape[1]
    ext = jnp.concatenate([hist, u], axis=1)
    ext32 = ext.astype(jnp.float32)
    cs = jnp.cumsum(ext32, axis=1)
    cs = jnp.concatenate([jnp.zeros_like(cs[:, :1]), cs], axis=1)
    end = cs[:, POOL_HIST + 1:]
    cur = ext32[:, POOL_HIST:]
    diffs = []
    for g, w in enumerate(POOL_WINDOWS):
        sl = slice(g * POOL_GROUP, (g + 1) * POOL_GROUP)
        start = cs[:, POOL_HIST + 1 - w:POOL_HIST + 1 - w + T, sl]
        cnt = jnp.minimum(pos + 1, w).astype(jnp.float32)[None, :, None]
        diffs.append((end[..., sl] - start) / cnt - cur[..., sl])
    d = jnp.stack(diffs, axis=-2).astype(u.dtype)
    y = jnp.einsum('btgc,gco->btgo', d, w_pool).reshape(u.shape[0], T, D_MODEL)
    return y * pool_scale, ext[:, -POOL_HIST:]


def conv_module(glu_in, hist, w_dw, b_dw, ln_g, ln_b, w_pw):
    a, gate = glu_in[..., :CONV_WIDTH], glu_in[..., CONV_WIDTH:]
    v = a * jax.nn.sigmoid(gate)
    ext = jnp.concatenate([hist, v], axis=1)
    y = depthwise_causal(ext, w_dw, b_dw)
    y = jax.nn.silu(layer_norm(y, ln_g, ln_b))
    return y @ w_pw, ext[:, -CONV_HIST:]


def _mla_attend(q_nope, q_rope, q_pos, ckv, krope, k_pos, w_uk, w_uv):
    q_lat = jnp.einsum('bqhn,chn->bqhc', q_nope, w_uk)
    s = (jnp.einsum('bqhc,bkc->bhqk', q_lat, ckv)
         + jnp.einsum('bqhr,bkr->bhqk', q_rope, krope)).astype(jnp.float32) * ATTN_SCALE
    visible = (k_pos[None, :] // CHUNK) <= (q_pos[:, None] // CHUNK)
    s = jnp.where(visible[None, None], s, NEG_INF)
    prob = jax.nn.softmax(s, axis=-1).astype(ckv.dtype)
    o_lat = jnp.einsum('bhqk,bkc->bqhc', prob, ckv)
    return jnp.einsum('bqhc,chv->bqhv', o_lat, w_uv)


def mla_attention(q_nope, q_rope, q_pos, ckv, krope, k_pos, w_uk, w_uv):
    B, T = q_nope.shape[:2]
    if T > Q_BLOCK and T % Q_BLOCK == 0:
        nb = T // Q_BLOCK
        qn = q_nope.reshape(B, nb, Q_BLOCK, N_HEADS, QK_NOPE).swapaxes(0, 1)
        qr = q_rope.reshape(B, nb, Q_BLOCK, N_HEADS, QK_ROPE).swapaxes(0, 1)
        qp = q_pos.reshape(nb, Q_BLOCK)

        def block(args):
            bn, br, bp = args
            return _mla_attend(bn, br, bp, ckv, krope, k_pos, w_uk, w_uv)

        o = lax.map(block, (qn, qr, qp))
        return o.swapaxes(0, 1).reshape(B, T, N_HEADS, V_DIM)
    return _mla_attend(q_nope, q_rope, q_pos, ckv, krope, k_pos, w_uk, w_uv)


def _trunk(x, pos0, cache_ckv, cache_krope, state_pool, state_conv, state_ffn,
           g_pre_mix, w_in, b_gate, g_q_a, g_kv_a, w_uq, w_uk, w_uv, w_o_mla,
           w_pool, pool_scale, w_conv_dw, b_conv_dw, g_conv_ln, b_conv_ln, w_conv_pw,
           w_out, g_post_mix, g_pre_ffn, w_up, w_ffn_dw, b_ffn_dw, w_down, g_post_ffn):
    B, T, _ = x.shape
    pos = pos0 + jnp.arange(T, dtype=jnp.int32)
    new_ckv, new_kr, new_pool, new_conv, new_ffn = [], [], [], [], []
    for l in range(DEPTH):
        h = rms_norm(x, g_pre_mix[l])
        p = h @ w_in[l]
        ph = jnp.zeros((B, POOL_HIST, POOL_WIDTH), x.dtype) if state_pool is None else state_pool[l]
        y_pool, st_pool = pool_mixer(p[..., OFF_POOL:OFF_GLU], ph, pos, w_pool[l], pool_scale[l])
        chh = jnp.zeros((B, CONV_HIST, CONV_WIDTH), x.dtype) if state_conv is None else state_conv[l]
        y_conv, st_conv = conv_module(p[..., OFF_GLU:OFF_Q], chh, w_conv_dw[l], b_conv_dw[l],
                                      g_conv_ln[l], b_conv_ln[l], w_conv_pw[l])
        c_q = rms_norm(p[..., OFF_Q:OFF_KV], g_q_a[l])
        c_kv = rms_norm(p[..., OFF_KV:OFF_KR], g_kv_a[l])
        k_r = apply_rope(p[..., OFF_KR:OFF_GATE], pos)
        q = jnp.einsum('btc,chd->bthd', c_q, w_uq[l])
        q_nope = q[..., :QK_NOPE]
        q_rope = apply_rope(q[..., QK_NOPE:], pos)
        if cache_ckv is None:
            ckv_all, kr_all, k_pos = c_kv, k_r, pos
        else:
            ckv_all = jnp.concatenate([cache_ckv[l], c_kv], axis=1)
            kr_all = jnp.concatenate([cache_krope[l], k_r], axis=1)
            k_pos = jnp.arange(cache_ckv.shape[2] + T, dtype=jnp.int32)
        o = mla_attention(q_nope, q_rope, pos, ckv_all, kr_all, k_pos, w_uk[l], w_uv[l])
        y_mla = o.reshape(B, T, N_HEADS * V_DIM) @ w_o_mla[l]
        gates = jax.nn.sigmoid((p[..., OFF_GATE:] + b_gate[l]).astype(jnp.float32)).astype(x.dtype)
        gates = gates.reshape(B, T, N_BRANCH, D_MODEL)
        merged = gates[:, :, 0] * y_pool + gates[:, :, 1] * y_conv + gates[:, :, 2] * y_mla
        x = x + rms_norm(merged @ w_out[l], g_post_mix[l])
        h2 = rms_norm(x, g_pre_ffn[l])
        up = h2 @ w_up[l]
        fh = jnp.zeros((B, FFN_HIST, 2 * D_FF), x.dtype) if state_ffn is None else state_ffn[l]
        ext = jnp.concatenate([fh, up], axis=1)
        uc = depthwise_causal(ext, w_ffn_dw[l], b_ffn_dw[l])
        f = jax.nn.gelu(uc[..., :D_FF], approximate=True) * uc[..., D_FF:]
        x = x + rms_norm(f @ w_down[l], g_post_ffn[l])
        new_ckv.append(c_kv)
        new_kr.append(k_r)
        new_pool.append(st_pool)
        new_conv.append(st_conv)
        new_ffn.append(ext[:, -FFN_HIST:])
    return (x, jnp.stack(new_ckv), jnp.stack(new_kr), jnp.stack(new_pool),
            jnp.stack(new_conv), jnp.stack(new_ffn))


def setup_inputs(seed: int = 0) -> dict:
    key = jax.random.key(seed)
    ks = iter(jax.random.split(key, 48))

    def nrm(shape, scale=1.0):
        return jax.random.normal(next(ks), shape, jnp.float32) * scale

    def gain(shape):
        return 1.0 + 0.05 * nrm(shape)

    L = DEPTH
    return {
        'x_prompt': nrm((BATCH, SEQ, D_MODEL)),
        'x_sample': nrm((DEC_BATCH, DEC_SEQ, D_MODEL)),
        'cache_ckv': nrm((L, DEC_BATCH, PAST_LEN, KV_LORA)),
        'cache_krope': nrm((L, DEC_BATCH, PAST_LEN, QK_ROPE)),
        'state_pool': nrm((L, DEC_BATCH, POOL_HIST, POOL_WIDTH)),
        'state_conv': nrm((L, DEC_BATCH, CONV_HIST, CONV_WIDTH)),
        'state_ffn': nrm((L, DEC_BATCH, FFN_HIST, 2 * D_FF)),
        'g_pre_mix': gain((L, D_MODEL)),
        'w_in': nrm((L, D_MODEL, N_IN), D_MODEL ** -0.5),
        'b_gate': nrm((L, N_BRANCH * D_MODEL), 0.02),
        'g_q_a': gain((L, Q_LORA)),
        'g_kv_a': gain((L, KV_LORA)),
        'w_uq': nrm((L, Q_LORA, N_HEADS, QK_NOPE + QK_ROPE), Q_LORA ** -0.5),
        'w_uk': nrm((L, KV_LORA, N_HEADS, QK_NOPE), KV_LORA ** -0.5),
        'w_uv': nrm((L, KV_LORA, N_HEADS, V_DIM), KV_LORA ** -0.5),
        'w_o_mla': nrm((L, N_HEADS * V_DIM, D_MODEL), (N_HEADS * V_DIM) ** -0.5),
        'w_pool': nrm((L, POOL_GROUPS, POOL_GROUP, POOL_OUT_GROUP), POOL_GROUP ** -0.5),
        'pool_scale': 1.0 + 0.1 * nrm((L, D_MODEL)),
        'w_conv_dw': nrm((L, CONV_K, CONV_WIDTH), CONV_K ** -0.5),
        'b_conv_dw': nrm((L, CONV_WIDTH), 0.02),
        'g_conv_ln': gain((L, CONV_WIDTH)),
        'b_conv_ln': nrm((L, CONV_WIDTH), 0.02),
        'w_conv_pw': nrm((L, CONV_WIDTH, D_MODEL), CONV_WIDTH ** -0.5),
        'w_out': nrm((L, D_MODEL, D_MODEL), D_MODEL ** -0.5),
        'g_post_mix': gain((L, D_MODEL)),
        'g_pre_ffn': gain((L, D_MODEL)),
        'w_up': nrm((L, D_MODEL, 2 * D_FF), D_MODEL ** -0.5),
        'w_ffn_dw': nrm((L, FFN_K, 2 * D_FF), FFN_K ** -0.5),
        'b_ffn_dw': nrm((L, 2 * D_FF), 0.02),
        'w_down': nrm((L, D_FF, D_MODEL), D_FF ** -0.5),
        'g_post_ffn': gain((L, D_MODEL)),
    }


def reference(x_prompt, x_sample, cache_ckv, cache_krope, state_pool, state_conv, state_ffn,
              g_pre_mix, w_in, b_gate, g_q_a, g_kv_a, w_uq, w_uk, w_uv, w_o_mla,
              w_pool, pool_scale, w_conv_dw, b_conv_dw, g_conv_ln, b_conv_ln, w_conv_pw,
              w_out, g_post_mix, g_pre_ffn, w_up, w_ffn_dw, b_ffn_dw, w_down, g_post_ffn):
    weights = (g_pre_mix, w_in, b_gate, g_q_a, g_kv_a, w_uq, w_uk, w_uv, w_o_mla,
               w_pool, pool_scale, w_conv_dw, b_conv_dw, g_conv_ln, b_conv_ln, w_conv_pw,
               w_out, g_post_mix, g_pre_ffn, w_up, w_ffn_dw, b_ffn_dw, w_down, g_post_ffn)
    y_prompt, p_ckv, p_krope, p_pool, p_conv, p_ffn = _trunk(
        x_prompt, 0, None, None, None, None, None, *weights)
    y_sample, s_ckv, s_krope, s_pool, s_conv, s_ffn = _trunk(
        x_sample, cache_ckv.shape[2], cache_ckv, cache_krope, state_pool, state_conv, state_ffn,
        *weights)
    return (y_prompt, y_sample, p_ckv, p_krope, p_pool, p_conv, p_ffn,
            s_ckv, s_krope, s_pool, s_conv, s_ffn)
```

```python
import functools
import math

import jax
import jax.numpy as jnp
from jax import lax
from jax.experimental import pallas as pl
from jax.experimental.pallas import tpu as pltpu

CHUNK = 64
EPS = 1e-6
NEG_INF = -1e30
POOL_WINDOWS = (2, 4, 8, 16)
POOL_HIST = max(POOL_WINDOWS) - 1
POOL_HIST_PAD = 16
CONV_K = 31
CONV_HIST = CONV_K - 1
CONV_HIST_PAD = 32
FFN_K = 3
FFN_HIST = FFN_K - 1
FFN_HIST_PAD = 8
ROPE_THETA = 10000.0
LANE = 128
V7X_VMEM_LIMIT = 56 * 1024 * 1024

F32 = jnp.float32
BF16 = jnp.bfloat16


def _params(n_axes):
    return pltpu.CompilerParams(dimension_semantics=("arbitrary",) * n_axes,
                                vmem_limit_bytes=V7X_VMEM_LIMIT)


def _const_spec(shape):
    zeros = (0,) * len(shape)
    return pl.BlockSpec(shape, lambda *_: zeros, pipeline_mode=pl.Buffered(1))


def _dot(a, b):
    return jnp.dot(a, b, preferred_element_type=F32)


def _dot_nt(a, b):
    return lax.dot_general(a, b, (((1,), (1,)), ((), ())), preferred_element_type=F32)


def _rms(x, g):
    return x * lax.rsqrt(jnp.mean(x * x, axis=-1, keepdims=True) + EPS) * g


def _pick(n, pref):
    t = min(n, pref)
    while n % t:
        t -= 1
    return t


def _norm_kernel(x_ref, g_ref, h_ref):
    h_ref[...] = _rms(x_ref[...], g_ref[...]).astype(BF16)


def _norm_cast(x, g):
    m, d = x.shape
    tm = _pick(m, 512)
    return pl.pallas_call(
        _norm_kernel,
        grid=(m // tm,),
        in_specs=[pl.BlockSpec((tm, d), lambda i: (i, 0)), _const_spec((1, d))],
        out_specs=pl.BlockSpec((tm, d), lambda i: (i, 0)),
        out_shape=jax.ShapeDtypeStruct((m, d), BF16),
        compiler_params=_params(1),
        name="norm_cast",
    )(x, g)


def _mm_kernel(x_ref, w_ref, o_ref):
    o_ref[...] = _dot(x_ref[...], w_ref[...]).astype(o_ref.dtype)


def _matmul(x, w, tm_pref=1024, tn_pref=1024):
    m, k = x.shape
    n = w.shape[1]
    tm = _pick(m, tm_pref)
    tn = _pick(n, tn_pref)
    return pl.pallas_call(
        _mm_kernel,
        grid=(m // tm, n // tn),
        in_specs=[pl.BlockSpec((tm, k), lambda i, j: (i, 0)),
                  pl.BlockSpec((k, tn), lambda i, j: (0, j))],
        out_specs=pl.BlockSpec((tm, tn), lambda i, j: (i, j)),
        out_shape=jax.ShapeDtypeStruct((m, n), BF16),
        compiler_params=_params(2),
        name="matmul",
    )(x, w)


def _mla_prep_kernel(h_ref, wb_ref, gq_ref, gkv_ref, cos_ref, sin_ref, wq_ref, *rest,
                     n_heads, q_lora, kv_lora, nope, rope, scale, expand_kv):
    if expand_kv:
        wuk_ref, wuv_ref, ckv_ref, kr_ref, q_ref, k_ref, v_ref = rest
    else:
        ckv_ref, kr_ref, q_ref = rest
    hn = n_heads * nope
    pb = _dot(h_ref[...], wb_ref[...])
    cq = _rms(pb[:, :q_lora], gq_ref[...])
    ckv = _rms(pb[:, q_lora:q_lora + kv_lora], gkv_ref[...])
    cos = cos_ref[...]
    sin = sin_ref[...]
    o = q_lora + kv_lora
    kr = pb[:, o:o + LANE] * cos + pb[:, o + LANE:o + 2 * LANE] * sin
    ckv_ref[...] = ckv
    kr_ref[...] = kr[:, :rope]
    qall = _dot(cq.astype(BF16), wq_ref[...])
    for hh in range(n_heads):
        c0 = hh * LANE
        qn = qall[:, c0:c0 + nope] * scale
        qr = (qall[:, hn + c0:hn + c0 + LANE] * cos
              + qall[:, 2 * hn + c0:2 * hn + c0 + LANE] * sin) * scale
        q_ref[:, 2 * c0:2 * c0 + nope] = qn.astype(BF16)
        q_ref[:, 2 * c0 + nope:2 * c0 + nope + LANE] = qr.astype(BF16)
    if expand_kv:
        ckvb = ckv.astype(BF16)
        kn = _dot(ckvb, wuk_ref[...])
        v_ref[...] = _dot(ckvb, wuv_ref[...]).astype(BF16)
        krb = kr.astype(BF16)
        for hh in range(n_heads):
            c0 = hh * LANE
            k_ref[:, 2 * c0:2 * c0 + nope] = kn[:, c0:c0 + nope].astype(BF16)
            k_ref[:, 2 * c0 + nope:2 * c0 + nope + LANE] = krb


def _mla_prep(h, wb, gq, gkv, cos_t, sin_t, wq, wuk, wuv, *, seq, n_heads, nope, rope, scale,
              expand_kv):
    m, d = h.shape
    q_lora, kv_lora = gq.shape[1], gkv.shape[1]
    assert nope == LANE and rope <= LANE
    tm = _pick(seq, 256)
    nt = seq // tm
    hq = n_heads * (nope + LANE)
    row = lambda i: (i, 0)
    tab = lambda i: (i % nt, 0)
    in_specs = [pl.BlockSpec((tm, d), row), _const_spec(wb.shape), _const_spec(gq.shape),
                _const_spec(gkv.shape), pl.BlockSpec((tm, LANE), tab), pl.BlockSpec((tm, LANE), tab),
                _const_spec(wq.shape)]
    args = [h, wb, gq, gkv, cos_t, sin_t, wq]
    out_specs = [pl.BlockSpec((tm, kv_lora), row), pl.BlockSpec((tm, rope), row),
                 pl.BlockSpec((tm, hq), row)]
    out_shape = [jax.ShapeDtypeStruct((m, kv_lora), F32), jax.ShapeDtypeStruct((m, rope), F32),
                 jax.ShapeDtypeStruct((m, hq), BF16)]
    if expand_kv:
        in_specs += [_const_spec(wuk.shape), _const_spec(wuv.shape)]
        args += [wuk, wuv]
        out_specs += [pl.BlockSpec((tm, hq), row), pl.BlockSpec((tm, n_heads * nope), row)]
        out_shape += [jax.ShapeDtypeStruct((m, hq), BF16),
                      jax.ShapeDtypeStruct((m, n_heads * nope), BF16)]
    return pl.pallas_call(
        functools.partial(_mla_prep_kernel, n_heads=n_heads, q_lora=q_lora, kv_lora=kv_lora,
                          nope=nope, rope=rope, scale=scale, expand_kv=expand_kv),
        grid=(m // tm,), in_specs=in_specs, out_specs=out_specs, out_shape=out_shape,
        compiler_params=_params(1), name="mla_prep",
    )(*args)


def _flash_kernel(q_ref, k_ref, v_ref, o_ref, *, tq):
    qi = pl.program_id(2)
    q = q_ref[...]
    dv = v_ref.shape[1]

    def step(start, carry, masked):
        m, l, acc = carry
        k = k_ref[pl.ds(start, tq), :]
        v = v_ref[pl.ds(start, tq), :]
        s = _dot_nt(q, k)
        if masked:
            r = lax.broadcasted_iota(jnp.int32, (tq, tq), 0) // CHUNK
            c = lax.broadcasted_iota(jnp.int32, (tq, tq), 1) // CHUNK
            s = jnp.where(c <= r, s, NEG_INF)
        m_new = jnp.maximum(m, jnp.max(s, axis=-1, keepdims=True))
        p = jnp.exp(s - m_new)
        alpha = jnp.exp(m - m_new)
        l = alpha * l + jnp.sum(p, axis=-1, keepdims=True)
        acc = alpha * acc + _dot(p.astype(BF16), v)
        return m_new, l, acc

    init = (jnp.full((tq, 1), NEG_INF, F32), jnp.zeros((tq, 1), F32), jnp.zeros((tq, dv), F32))
    carry = lax.fori_loop(
        0, qi, lambda j, c: step(pl.multiple_of(j * tq, tq), c, False), init)
    _, l, acc = step(pl.multiple_of(qi * tq, tq), carry, True)
    o_ref[...] = (acc / l).astype(BF16)


def _flash(q, k, v, *, batch, seq, n_heads, nope):
    m = q.shape[0]
    tq = _pick(seq, 512)
    assert tq % CHUNK == 0
    nq = seq // tq
    return pl.pallas_call(
        functools.partial(_flash_kernel, tq=tq),
        grid=(batch, n_heads, nq),
        in_specs=[pl.BlockSpec((tq, 2 * LANE), lambda b, h, i: (b * nq + i, h)),
                  pl.BlockSpec((seq, 2 * LANE), lambda b, h, i: (b, h)),
                  pl.BlockSpec((seq, nope), lambda b, h, i: (b, h))],
        out_specs=pl.BlockSpec((tq, nope), lambda b, h, i: (b * nq + i, h)),
        out_shape=jax.ShapeDtypeStruct((m, n_heads * nope), BF16),
        compiler_params=_params(3), name="flash",
    )(q, k, v)


def _cached_attn_kernel(q_ref, cc_ref, kc_ref, cn_ref, kn_ref, wuk_ref, wuv_ref, o_ref,
                        qlat_scr, qr_scr, *, n_heads, nope, rope, seq):
    ckv_c = cc_ref[...].astype(BF16)
    kr_c = kc_ref[...].astype(BF16)
    ckv_n = cn_ref[...].astype(BF16)
    kr_n = kn_ref[...].astype(BF16)
    for hh in range(n_heads):
        c0 = hh * 2 * LANE
        qlat_scr[hh * seq:(hh + 1) * seq, :] = _dot(q_ref[:, c0:c0 + nope], wuk_ref[hh]).astype(BF16)
        qr_scr[hh * seq:(hh + 1) * seq, :] = q_ref[:, c0 + nope:c0 + nope + rope]
    qlat = qlat_scr[...]
    qr = qr_scr[...]
    s_c = _dot_nt(qlat, ckv_c) + _dot_nt(qr, kr_c)
    s_n = _dot_nt(qlat, ckv_n) + _dot_nt(qr, kr_n)
    mx = jnp.maximum(jnp.max(s_c, axis=-1, keepdims=True), jnp.max(s_n, axis=-1, keepdims=True))
    p_c = jnp.exp(s_c - mx)
    p_n = jnp.exp(s_n - mx)
    den = jnp.sum(p_c, axis=-1, keepdims=True) + jnp.sum(p_n, axis=-1, keepdims=True)
    o_lat = ((_dot(p_c.astype(BF16), ckv_c) + _dot(p_n.astype(BF16), ckv_n)) / den).astype(BF16)
    for hh in range(n_heads):
        o_ref[:, hh * nope:(hh + 1) * nope] = _dot(
            o_lat[hh * seq:(hh + 1) * seq, :], wuv_ref[hh]).astype(BF16)


def _cached_attn(q, cache_ckv, cache_kr, ckv_new, kr_new, wuk_h, wuv_h, *, layer, batch, seq,
                 n_heads, nope, rope):
    past, kv_lora = cache_ckv.shape[2], cache_ckv.shape[3]
    assert seq <= CHUNK and past % CHUNK == 0
    hq = q.shape[1]
    return pl.pallas_call(
        functools.partial(_cached_attn_kernel, n_heads=n_heads, nope=nope, rope=rope, seq=seq),
        grid=(batch,),
        in_specs=[pl.BlockSpec((seq, hq), lambda b: (b, 0)),
                  pl.BlockSpec((None, None, past, kv_lora), lambda b: (layer, b, 0, 0)),
                  pl.BlockSpec((None, None, past, rope), lambda b: (layer, b, 0, 0)),
                  pl.BlockSpec((seq, kv_lora), lambda b: (b, 0)),
                  pl.BlockSpec((seq, rope), lambda b: (b, 0)),
                  _const_spec(wuk_h.shape), _const_spec(wuv_h.shape)],
        out_specs=pl.BlockSpec((seq, n_heads * nope), lambda b: (b, 0)),
        out_shape=jax.ShapeDtypeStruct((batch * seq, n_heads * nope), BF16),
        scratch_shapes=[pltpu.VMEM((n_heads * seq, kv_lora), BF16),
                        pltpu.VMEM((n_heads * seq, rope), BF16)],
        compiler_params=_params(1), name="cached_attn",
    )(q, cache_ckv, cache_kr, ckv_new, kr_new, wuk_h, wuv_h)


def _pool_kernel(u_ref, hist_ref, w_ref, scale_ref, y_ref, st_ref, ext, *, tt, pos0):
    ti = pl.program_id(1)
    hp = POOL_HIST_PAD

    @pl.when(ti == 0)
    def _():
        ext[0:hp, :] = hist_ref[...]

    @pl.when(ti > 0)
    def _():
        ext[0:hp, :] = ext[tt:tt + hp, :]

    ext[hp:hp + tt, :] = u_ref[...].astype(F32)
    pos = pos0 + ti * tt + lax.broadcasted_iota(jnp.int32, (tt, 1), 0)
    pg = w_ref.shape[1]
    og = w_ref.shape[2]
    for g, w in enumerate(POOL_WINDOWS):
        c0, c1 = g * pg, (g + 1) * pg
        cur = ext[hp:hp + tt, c0:c1]
        tot = cur
        for k in range(1, w):
            tot = tot + ext[hp - k:hp - k + tt, c0:c1]
        cnt = jnp.minimum(pos + 1, w).astype(F32)
        d = tot / cnt - cur
        y = _dot(d.astype(BF16), w_ref[g]) * scale_ref[:, g * og:(g + 1) * og]
        y_ref[:, g * og:(g + 1) * og] = y.astype(BF16)
    st_ref[...] = ext[tt:tt + hp, :]


def _pool(pa, hist, w_pool, scale, *, batch, seq, col0, pos0):
    m = pa.shape[0]
    g, pg, og = w_pool.shape
    width = g * pg
    assert col0 % width == 0
    tt = _pick(seq, 256)
    assert tt >= POOL_HIST_PAD
    nt = seq // tt
    return pl.pallas_call(
        functools.partial(_pool_kernel, tt=tt, pos0=pos0),
        grid=(batch, nt),
        in_specs=[pl.BlockSpec((tt, width), lambda b, t: (b * nt + t, col0 // width)),
                  pl.BlockSpec((None, POOL_HIST_PAD, width), lambda b, t: (b, 0, 0)),
                  _const_spec(w_pool.shape), _const_spec(scale.shape)],
        out_specs=[pl.BlockSpec((tt, g * og), lambda b, t: (b * nt + t, 0)),
                   pl.BlockSpec((None, POOL_HIST_PAD, width), lambda b, t: (b, 0, 0))],
        out_shape=[jax.ShapeDtypeStruct((m, g * og), BF16),
                   jax.ShapeDtypeStruct((batch, POOL_HIST_PAD, width), F32)],
        scratch_shapes=[pltpu.VMEM((POOL_HIST_PAD + tt, width), F32)],
        compiler_params=_params(2), name="pool",
    )(pa, hist, w_pool, scale)


def _conv_kernel(a_ref, gate_ref, hist_ref, wdw_ref, bdw_ref, lng_ref, lnb_ref, wpw_ref,
                 y_ref, st_ref, ext, *, tt):
    ti = pl.program_id(1)
    hp = CONV_HIST_PAD

    @pl.when(ti == 0)
    def _():
        ext[0:hp, :] = hist_ref[...]

    @pl.when(ti > 0)
    def _():
        ext[0:hp, :] = ext[tt:tt + hp, :]

    ext[hp:hp + tt, :] = a_ref[...].astype(F32) * jax.nn.sigmoid(gate_ref[...].astype(F32))
    off = hp - CONV_HIST
    acc = bdw_ref[...] + wdw_ref[0:1, :] * ext[off:off + tt, :]
    for k in range(1, CONV_K):
        acc = acc + wdw_ref[k:k + 1, :] * ext[off + k:off + k + tt, :]
    mu = jnp.mean(acc, axis=-1, keepdims=True)
    cen = acc - mu
    var = jnp.mean(cen * cen, axis=-1, keepdims=True)
    y = cen * lax.rsqrt(var + EPS) * lng_ref[...] + lnb_ref[...]
    y = y * jax.nn.sigmoid(y)
    y_ref[...] = _dot(y.astype(BF16), wpw_ref[...]).astype(BF16)
    st_ref[...] = ext[tt:tt + hp, :]


def _conv(pa, hist, wdw, bdw, lng, lnb, wpw, *, batch, seq, col0):
    m = pa.shape[0]
    width, d = wpw.shape
    assert col0 % width == 0
    tt = _pick(seq, 256)
    assert tt >= CONV_HIST_PAD
    nt = seq // tt
    cb = col0 // width
    return pl.pallas_call(
        functools.partial(_conv_kernel, tt=tt),
        grid=(batch, nt),
        in_specs=[pl.BlockSpec((tt, width), lambda b, t: (b * nt + t, cb)),
                  pl.BlockSpec((tt, width), lambda b, t: (b * nt + t, cb + 1)),
                  pl.BlockSpec((None, CONV_HIST_PAD, width), lambda b, t: (b, 0, 0)),
                  _const_spec(wdw.shape), _const_spec(bdw.shape), _const_spec(lng.shape),
                  _const_spec(lnb.shape), _const_spec(wpw.shape)],
        out_specs=[pl.BlockSpec((tt, d), lambda b, t: (b * nt + t, 0)),
                   pl.BlockSpec((None, CONV_HIST_PAD, width), lambda b, t: (b, 0, 0))],
        out_shape=[jax.ShapeDtypeStruct((m, d), BF16),
                   jax.ShapeDtypeStruct((batch, CONV_HIST_PAD, width), F32)],
        scratch_shapes=[pltpu.VMEM((CONV_HIST_PAD + tt, width), F32)],
        compiler_params=_params(2), name="conv",
    )(pa, pa, hist, wdw, bdw, lng, lnb, wpw)


def _merge_kernel(g0_ref, g1_ref, g2_ref, bg_ref, yp_ref, yc_ref, o_ref, wo_ref, wout_ref, x_ref,
                  gpost_ref, gnext_ref, xo_ref, h_ref):
    d = x_ref.shape[1]
    bg = bg_ref[...]
    gate0 = jax.nn.sigmoid(g0_ref[...].astype(F32) + bg[:, 0:d])
    gate1 = jax.nn.sigmoid(g1_ref[...].astype(F32) + bg[:, d:2 * d])
    gate2 = jax.nn.sigmoid(g2_ref[...].astype(F32) + bg[:, 2 * d:3 * d])
    y_mla = _dot(o_ref[...], wo_ref[...])
    merged = gate0 * yp_ref[...].astype(F32) + gate1 * yc_ref[...].astype(F32) + gate2 * y_mla
    z = _dot(merged.astype(BF16), wout_ref[...])
    xn = x_ref[...] + _rms(z, gpost_ref[...])
    xo_ref[...] = xn
    h_ref[...] = _rms(xn, gnext_ref[...]).astype(BF16)


def _merge(pa, bg, yp, yc, o, wo, wout, x, gpost, gnext):
    m, d = x.shape
    tm = _pick(m, 256)
    row = lambda i: (i, 0)
    return pl.pallas_call(
        _merge_kernel,
        grid=(m // tm,),
        in_specs=[pl.BlockSpec((tm, d), lambda i: (i, 0)), pl.BlockSpec((tm, d), lambda i: (i, 1)),
                  pl.BlockSpec((tm, d), lambda i: (i, 2)), _const_spec(bg.shape),
                  pl.BlockSpec((tm, d), row), pl.BlockSpec((tm, d), row),
                  pl.BlockSpec((tm, o.shape[1]), row), _const_spec(wo.shape),
                  _const_spec(wout.shape), pl.BlockSpec((tm, d), row),
                  _const_spec(gpost.shape), _const_spec(gnext.shape)],
        out_specs=[pl.BlockSpec((tm, d), row), pl.BlockSpec((tm, d), row)],
        out_shape=[jax.ShapeDtypeStruct((m, d), F32), jax.ShapeDtypeStruct((m, d), BF16)],
        compiler_params=_params(1), name="merge",
    )(pa, pa, pa, bg, yp, yc, o, wo, wout, x, gpost, gnext)


def _ffn_kernel(ug_ref, uv_ref, hg_ref, hv_ref, wg_ref, wv_ref, bgc_ref, bvc_ref, wd_ref, x_ref,
                gpost_ref, gnext_ref, xo_ref, h_ref, sg_ref, sv_ref, acc, carry, ext_g, ext_v,
                *, tm, nt, nk):
    i = pl.program_id(0)
    kk = pl.program_id(1)
    hp = FFN_HIST_PAD
    first = (i % nt) == 0

    @pl.when(first)
    def _():
        ext_g[0:hp, :] = hg_ref[...]
        ext_v[0:hp, :] = hv_ref[...]

    @pl.when(jnp.logical_not(first))
    def _():
        ext_g[0:hp, :] = carry[kk, 0]
        ext_v[0:hp, :] = carry[kk, 1]

    ext_g[hp:hp + tm, :] = ug_ref[...].astype(F32)
    ext_v[hp:hp + tm, :] = uv_ref[...].astype(F32)
    off = hp - FFN_HIST

    def dw(ext, w_ref, b_ref):
        y = b_ref[...] + w_ref[0:1, :] * ext[off:off + tm, :]
        for k in range(1, FFN_K):
            y = y + w_ref[k:k + 1, :] * ext[off + k:off + k + tm, :]
        return y

    f = jax.nn.gelu(dw(ext_g, wg_ref, bgc_ref), approximate=True) * dw(ext_v, wv_ref, bvc_ref)

    @pl.when(kk == 0)
    def _():
        acc[...] = jnp.zeros_like(acc)

    acc[...] += _dot(f.astype(BF16), wd_ref[...])
    tail_g = ext_g[tm:tm + hp, :]
    tail_v = ext_v[tm:tm + hp, :]
    carry[kk, 0] = tail_g
    carry[kk, 1] = tail_v
    sg_ref[...] = tail_g
    sv_ref[...] = tail_v

    @pl.when(kk == nk - 1)
    def _():
        xn = x_ref[...] + _rms(acc[...], gpost_ref[...])
        xo_ref[...] = xn
        h_ref[...] = _rms(xn, gnext_ref[...]).astype(BF16)


def _ffn(up, hist_g, hist_v, wdw, bdw, wd, x, gpost, gnext, *, batch, seq):
    m, d = x.shape
    f = wd.shape[0]
    tm = _pick(seq, 512)
    assert tm >= FFN_HIST_PAD
    nt = seq // tm
    tk = _pick(f, 512)
    nk = f // tk
    row = lambda i, k: (i, 0)
    return pl.pallas_call(
        functools.partial(_ffn_kernel, tm=tm, nt=nt, nk=nk),
        grid=(m // tm, nk),
        in_specs=[pl.BlockSpec((tm, tk), lambda i, k: (i, k)),
                  pl.BlockSpec((tm, tk), lambda i, k: (i, k + nk)),
                  pl.BlockSpec((None, FFN_HIST_PAD, tk), lambda i, k: (i // nt, 0, k)),
                  pl.BlockSpec((None, FFN_HIST_PAD, tk), lambda i, k: (i // nt, 0, k)),
                  pl.BlockSpec((FFN_HIST_PAD, tk), lambda i, k: (0, k)),
                  pl.BlockSpec((FFN_HIST_PAD, tk), lambda i, k: (0, k + nk)),
                  pl.BlockSpec((1, tk), lambda i, k: (0, k)),
                  pl.BlockSpec((1, tk), lambda i, k: (0, k + nk)),
                  pl.BlockSpec((tk, d), lambda i, k: (k, 0)),
                  pl.BlockSpec((tm, d), row), _const_spec(gpost.shape), _const_spec(gnext.shape)],
        out_specs=[pl.BlockSpec((tm, d), row), pl.BlockSpec((tm, d), row),
                   pl.BlockSpec((None, FFN_HIST_PAD, tk), lambda i, k: (i, 0, k)),
                   pl.BlockSpec((None, FFN_HIST_PAD, tk), lambda i, k: (i, 0, k))],
        out_shape=[jax.ShapeDtypeStruct((m, d), F32), jax.ShapeDtypeStruct((m, d), BF16),
                   jax.ShapeDtypeStruct((m // tm, FFN_HIST_PAD, f), F32),
                   jax.ShapeDtypeStruct((m // tm, FFN_HIST_PAD, f), F32)],
        scratch_shapes=[pltpu.VMEM((tm, d), F32), pltpu.VMEM((nk, 2, FFN_HIST_PAD, tk), F32),
                        pltpu.VMEM((FFN_HIST_PAD + tm, tk), F32),
                        pltpu.VMEM((FFN_HIST_PAD + tm, tk), F32)],
        compiler_params=_params(2), name="ffn",
    )(up, up, hist_g, hist_v, wdw, wdw, bdw, bdw, wd, x, gpost, gnext)


def _rotate_half_cols(w):
    half = w.shape[-1] // 2
    return jnp.concatenate([-w[..., half:], w[..., :half]], axis=-1)


def _pad_last(w, n):
    return jnp.pad(w, [(0, 0)] * (w.ndim - 1) + [(0, n - w.shape[-1])])


def _prep_layer(l, dims, w_in, w_uq, w_uk, w_uv, w_o_mla, w_pool, w_conv_dw, w_conv_pw, w_out,
                w_up, w_ffn_dw, w_down):
    d, pw, cw, ql, kl, rope, nope, nh = dims
    o_glu = pw
    o_q = o_glu + 2 * cw
    o_kv = o_q + ql
    o_kr = o_kv + kl
    o_gate = o_kr + rope
    wi = w_in[l]
    wa = jnp.concatenate([wi[:, o_gate:], wi[:, o_glu:o_q], wi[:, :o_glu]], axis=1).astype(BF16)
    wkr = wi[:, o_kr:o_gate]
    wb = jnp.concatenate([wi[:, o_q:o_kr], _pad_last(wkr, LANE),
                          _pad_last(_rotate_half_cols(wkr), LANE)], axis=1).astype(BF16)
    uq = w_uq[l]
    uq_r = uq[..., nope:]
    wq = jnp.concatenate([uq[..., :nope].reshape(ql, nh * nope),
                          _pad_last(uq_r, LANE).reshape(ql, nh * LANE),
                          _pad_last(_rotate_half_cols(uq_r), LANE).reshape(ql, nh * LANE)],
                         axis=1).astype(BF16)
    uk, uv = w_uk[l], w_uv[l]
    return dict(
        wa=wa, wb=wb, wq=wq,
        wuk=uk.reshape(kl, -1).astype(BF16), wuv=uv.reshape(kl, -1).astype(BF16),
        wuk_h=jnp.transpose(uk, (1, 2, 0)).astype(BF16), wuv_h=jnp.transpose(uv, (1, 0, 2)).astype(BF16),
        wo=w_o_mla[l].astype(BF16), wpool=w_pool[l].astype(BF16),
        wdw=_pad_rows(w_conv_dw[l], CONV_HIST_PAD), wpw=w_conv_pw[l].astype(BF16),
        wout=w_out[l].astype(BF16), wup=w_up[l].astype(BF16),
        wfdw=_pad_rows(w_ffn_dw[l], FFN_HIST_PAD), wdown=w_down[l].astype(BF16))


def _pad_rows(w, n):
    return jnp.pad(w, [(0, n - w.shape[0])] + [(0, 0)] * (w.ndim - 1))


def _pad_front(s, n):
    return jnp.pad(s, [(0, 0), (n - s.shape[1], 0), (0, 0)])


def _rope_tables(pos0, seq, rope):
    half = rope // 2
    inv = ROPE_THETA ** (-jnp.arange(half, dtype=F32) / half)
    pos = pos0 + jnp.arange(seq, dtype=jnp.int32)
    ang = pos.astype(F32)[:, None] * inv[None, :]
    cos, sin = jnp.cos(ang), jnp.sin(ang)
    return (_pad_last(jnp.concatenate([cos, cos], axis=1), LANE),
            _pad_last(jnp.concatenate([sin, sin], axis=1), LANE))


def _trunk(x3, pos0, cache_ckv, cache_kr, state_pool, state_conv, state_ffn, layers, vecs, dims):
    batch, seq, d = x3.shape
    _, pw, cw, ql, kl, rope, nope, nh = dims
    depth = len(layers)
    m = batch * seq
    x = x3.reshape(m, d)
    cached = cache_ckv is not None
    scale = 1.0 / math.sqrt(nope + rope)
    cos_t, sin_t = _rope_tables(pos0, seq, rope)
    col_glu = 3 * d
    col_pool = 3 * d + 2 * cw
    row = lambda v: v.reshape(1, -1)
    h = _norm_cast(x, row(vecs["g_pre_mix"][0]))
    outs = {k: [] for k in ("ckv", "kr", "pool", "conv", "ffn")}
    for l in range(depth):
        w = layers[l]
        pa = _matmul(h, w["wa"])
        res = _mla_prep(h, w["wb"], row(vecs["g_q_a"][l]), row(vecs["g_kv_a"][l]), cos_t, sin_t,
                        w["wq"], w["wuk"], w["wuv"], seq=seq, n_heads=nh, nope=nope, rope=rope,
                        scale=scale, expand_kv=not cached)
        if cached:
            ckv, kr, q = res
            o = _cached_attn(q, cache_ckv, cache_kr, ckv, kr, w["wuk_h"], w["wuv_h"], layer=l,
                             batch=batch, seq=seq, n_heads=nh, nope=nope, rope=rope)
            hist_pool = _pad_front(state_pool[l], POOL_HIST_PAD)
            hist_conv = _pad_front(state_conv[l], CONV_HIST_PAD)
            hist_ffn = _pad_front(state_ffn[l], FFN_HIST_PAD)
        else:
            ckv, kr, q, kfull, vfull = res
            o = _flash(q, kfull, vfull, batch=batch, seq=seq, n_heads=nh, nope=nope)
            hist_pool = jnp.zeros((batch, POOL_HIST_PAD, pw), F32)
            hist_conv = jnp.zeros((batch, CONV_HIST_PAD, cw), F32)
            hist_ffn = jnp.zeros((batch, FFN_HIST_PAD, w["wfdw"].shape[1]), F32)
        yp, st_pool = _pool(pa, hist_pool, w["wpool"], row(vecs["pool_scale"][l]), batch=batch,
                            seq=seq, col0=col_pool, pos0=pos0)
        yc, st_conv = _conv(pa, hist_conv, w["wdw"], row(vecs["b_conv_dw"][l]),
                            row(vecs["g_conv_ln"][l]), row(vecs["b_conv_ln"][l]), w["wpw"],
                            batch=batch, seq=seq, col0=col_glu)
        x, h2 = _merge(pa, row(vecs["b_gate"][l]), yp, yc, o, w["wo"], w["wout"], x,
                       row(vecs["g_post_mix"][l]), row(vecs["g_pre_ffn"][l]))
        up = _matmul(h2, w["wup"])
        f = w["wdown"].shape[0]
        g_next = vecs["g_pre_mix"][(l + 1) % depth]
        x, h, st_g, st_v = _ffn(up, hist_ffn[..., :f], hist_ffn[..., f:], w["wfdw"],
                                row(vecs["b_ffn_dw"][l]), w["wdown"], x,
                                row(vecs["g_post_ffn"][l]), row(g_next), batch=batch, seq=seq)
        outs["ckv"].append(ckv.reshape(batch, seq, kl))
        outs["kr"].append(kr.reshape(batch, seq, rope))
        outs["pool"].append(st_pool[:, POOL_HIST_PAD - POOL_HIST:])
        outs["conv"].append(st_conv[:, CONV_HIST_PAD - CONV_HIST:])
        tails = jnp.concatenate([st_g, st_v], axis=-1)
        tails = tails.reshape(batch, -1, FFN_HIST_PAD, 2 * f)[:, -1]
        outs["ffn"].append(tails[:, FFN_HIST_PAD - FFN_HIST:])
    return (x.reshape(batch, seq, d),) + tuple(jnp.stack(outs[k]) for k in ("ckv", "kr", "pool", "conv", "ffn"))


def kernel(x_prompt, x_sample, cache_ckv, cache_krope, state_pool, state_conv, state_ffn,
           g_pre_mix, w_in, b_gate, g_q_a, g_kv_a, w_uq, w_uk, w_uv, w_o_mla,
           w_pool, pool_scale, w_conv_dw, b_conv_dw, g_conv_ln, b_conv_ln, w_conv_pw,
           w_out, g_post_mix, g_pre_ffn, w_up, w_ffn_dw, b_ffn_dw, w_down, g_post_ffn):
    depth, d = g_pre_mix.shape
    nh = w_uq.shape[2]
    nope = w_uk.shape[3]
    rope = w_uq.shape[3] - nope
    dims = (d, state_pool.shape[3], state_conv.shape[3], w_uq.shape[1], w_uk.shape[1], rope, nope, nh)
    layers = [_prep_layer(l, dims, w_in, w_uq, w_uk, w_uv, w_o_mla, w_pool, w_conv_dw, w_conv_pw,
                          w_out, w_up, w_ffn_dw, w_down) for l in range(depth)]
    vecs = dict(g_pre_mix=g_pre_mix, b_gate=b_gate, g_q_a=g_q_a, g_kv_a=g_kv_a,
                pool_scale=pool_scale, b_conv_dw=b_conv_dw, g_conv_ln=g_conv_ln,
                b_conv_ln=b_conv_ln, g_post_mix=g_post_mix, g_pre_ffn=g_pre_ffn,
                b_ffn_dw=b_ffn_dw, g_post_ffn=g_post_ffn)
    prompt = _trunk(x_prompt, 0, None, None, None, None, None, layers, vecs, dims)
    sample = _trunk(x_sample, cache_ckv.shape[2], cache_ckv, cache_krope, state_pool, state_conv,
                    state_ffn, layers, vecs, dims)
    return (prompt[0], sample[0]) + prompt[1:] + sample[1:]
```

```python
import functools
import math

import jax
import jax.numpy as jnp
from jax import lax
from jax.experimental import pallas as pl
from jax.experimental.pallas import tpu as pltpu

CHUNK = 64
EPS = 1e-6
NEG_INF = -1e30
POOL_WINDOWS = (2, 4, 8, 16)
POOL_HIST = max(POOL_WINDOWS) - 1
POOL_HIST_PAD = 16
CONV_K = 31
CONV_HIST = CONV_K - 1
CONV_HIST_PAD = 32
FFN_K = 3
FFN_HIST = FFN_K - 1
FFN_HIST_PAD = 8
FFN_ROW_CHUNK = 512
ROPE_THETA = 10000.0
LANE = 128
V7X_VMEM_LIMIT = 56 * 1024 * 1024

F32 = jnp.float32
BF16 = jnp.bfloat16


def _params(n_axes):
    return pltpu.CompilerParams(dimension_semantics=("arbitrary",) * n_axes,
                                vmem_limit_bytes=V7X_VMEM_LIMIT)


def _const_spec(shape):
    zeros = (0,) * len(shape)
    return pl.BlockSpec(shape, lambda *_: zeros, pipeline_mode=pl.Buffered(1))


def _dot(a, b):
    return jnp.dot(a, b, preferred_element_type=F32)


def _dot_nt(a, b):
    return lax.dot_general(a, b, (((1,), (1,)), ((), ())), preferred_element_type=F32)


def _rms(x, g):
    return x * lax.rsqrt(jnp.mean(x * x, axis=-1, keepdims=True) + EPS) * g


def _pick(n, pref):
    t = min(n, pref)
    while n % t:
        t -= 1
    return t


def _norm_kernel(x_ref, g_ref, h_ref):
    h_ref[...] = _rms(x_ref[...], g_ref[...]).astype(BF16)


def _norm_cast(x, g):
    m, d = x.shape
    tm = _pick(m, 512)
    return pl.pallas_call(
        _norm_kernel,
        grid=(m // tm,),
        in_specs=[pl.BlockSpec((tm, d), lambda i: (i, 0)), _const_spec((1, d))],
        out_specs=pl.BlockSpec((tm, d), lambda i: (i, 0)),
        out_shape=jax.ShapeDtypeStruct((m, d), BF16),
        compiler_params=_params(1),
        name="norm_cast",
    )(x, g)


def _mm_kernel(x_ref, w_ref, o_ref):
    o_ref[...] = _dot(x_ref[...], w_ref[...]).astype(o_ref.dtype)


def _matmul(x, w, tm_pref=1024, tn_pref=1024):
    m, k = x.shape
    n = w.shape[1]
    tm = _pick(m, tm_pref)
    tn = _pick(n, tn_pref)
    return pl.pallas_call(
        _mm_kernel,
        grid=(m // tm, n // tn),
        in_specs=[pl.BlockSpec((tm, k), lambda i, j: (i, 0)),
                  pl.BlockSpec((k, tn), lambda i, j: (0, j))],
        out_specs=pl.BlockSpec((tm, tn), lambda i, j: (i, j)),
        out_shape=jax.ShapeDtypeStruct((m, n), BF16),
        compiler_params=_params(2),
        name="matmul",
    )(x, w)


def _mla_prep_kernel(h_ref, wb_ref, gq_ref, gkv_ref, cos_ref, sin_ref, wq_ref, *rest,
                     n_heads, q_lora, kv_lora, nope, rope, scale, expand_kv):
    if expand_kv:
        wuk_ref, wuv_ref, ckv_ref, kr_ref, q_ref, k_ref, v_ref = rest
    else:
        ckv_ref, kr_ref, q_ref = rest
    hn = n_heads * nope
    pb = _dot(h_ref[...], wb_ref[...])
    cq = _rms(pb[:, :q_lora], gq_ref[...])
    ckv = _rms(pb[:, q_lora:q_lora + kv_lora], gkv_ref[...])
    cos = cos_ref[...]
    sin = sin_ref[...]
    o = q_lora + kv_lora
    kr = pb[:, o:o + LANE] * cos + pb[:, o + LANE:o + 2 * LANE] * sin
    ckv_ref[...] = ckv
    kr_ref[...] = kr[:, :rope]
    qall = _dot(cq.astype(BF16), wq_ref[...])
    for hh in range(n_heads):
        c0 = hh * LANE
        qn = qall[:, c0:c0 + nope] * scale
        qr = (qall[:, hn + c0:hn + c0 + LANE] * cos
              + qall[:, 2 * hn + c0:2 * hn + c0 + LANE] * sin) * scale
        q_ref[:, 2 * c0:2 * c0 + nope] = qn.astype(BF16)
        q_ref[:, 2 * c0 + nope:2 * c0 + nope + LANE] = qr.astype(BF16)
    if expand_kv:
        ckvb = ckv.astype(BF16)
        kn = _dot(ckvb, wuk_ref[...])
        v_ref[...] = _dot(ckvb, wuv_ref[...]).astype(BF16)
        krb = kr.astype(BF16)
        for hh in range(n_heads):
            c0 = hh * LANE
            k_ref[:, 2 * c0:2 * c0 + nope] = kn[:, c0:c0 + nope].astype(BF16)
            k_ref[:, 2 * c0 + nope:2 * c0 + nope + LANE] = krb


def _mla_prep(h, wb, gq, gkv, cos_t, sin_t, wq, wuk, wuv, *, seq, n_heads, nope, rope, scale,
              expand_kv):
    m, d = h.shape
    q_lora, kv_lora = gq.shape[1], gkv.shape[1]
    assert nope == LANE and rope <= LANE
    tm = _pick(seq, 256)
    nt = seq // tm
    hq = n_heads * (nope + LANE)
    row = lambda i: (i, 0)
    tab = lambda i: (i % nt, 0)
    in_specs = [pl.BlockSpec((tm, d), row), _const_spec(wb.shape), _const_spec(gq.shape),
                _const_spec(gkv.shape), pl.BlockSpec((tm, LANE), tab), pl.BlockSpec((tm, LANE), tab),
                _const_spec(wq.shape)]
    args = [h, wb, gq, gkv, cos_t, sin_t, wq]
    out_specs = [pl.BlockSpec((tm, kv_lora), row), pl.BlockSpec((tm, rope), row),
                 pl.BlockSpec((tm, hq), row)]
    out_shape = [jax.ShapeDtypeStruct((m, kv_lora), F32), jax.ShapeDtypeStruct((m, rope), F32),
                 jax.ShapeDtypeStruct((m, hq), BF16)]
    if expand_kv:
        in_specs += [_const_spec(wuk.shape), _const_spec(wuv.shape)]
        args += [wuk, wuv]
        out_specs += [pl.BlockSpec((tm, hq), row), pl.BlockSpec((tm, n_heads * nope), row)]
        out_shape += [jax.ShapeDtypeStruct((m, hq), BF16),
                      jax.ShapeDtypeStruct((m, n_heads * nope), BF16)]
    return pl.pallas_call(
        functools.partial(_mla_prep_kernel, n_heads=n_heads, q_lora=q_lora, kv_lora=kv_lora,
                          nope=nope, rope=rope, scale=scale, expand_kv=expand_kv),
        grid=(m // tm,), in_specs=in_specs, out_specs=out_specs, out_shape=out_shape,
        compiler_params=_params(1), name="mla_prep",
    )(*args)


def _flash_kernel(q_ref, k_ref, v_ref, o_ref, *, tq, heads, dv):
    dk = 2 * LANE
    nq = q_ref.shape[0] // tq
    r = lax.broadcasted_iota(jnp.int32, (tq, tq), 0) // CHUNK
    c = lax.broadcasted_iota(jnp.int32, (tq, tq), 1) // CHUNK
    visible = c <= r
    for qi in range(nq):
        rows = slice(qi * tq, (qi + 1) * tq)
        carry = [(jnp.full((tq, 1), NEG_INF, F32), jnp.zeros((tq, 1), F32),
                  jnp.zeros((tq, dv), F32)) for _ in range(heads)]
        for j in range(qi + 1):
            keys = slice(j * tq, (j + 1) * tq)
            for hh in range(heads):
                m, l, acc = carry[hh]
                s = _dot_nt(q_ref[rows, hh * dk:(hh + 1) * dk],
                            k_ref[keys, hh * dk:(hh + 1) * dk])
                if j == qi:
                    s = jnp.where(visible, s, NEG_INF)
                m_new = jnp.maximum(m, jnp.max(s, axis=-1, keepdims=True))
                p = jnp.exp2(s - m_new)
                alpha = jnp.exp2(m - m_new)
                l = alpha * l + jnp.sum(p, axis=-1, keepdims=True)
                acc = alpha * acc + _dot(p.astype(BF16), v_ref[keys, hh * dv:(hh + 1) * dv])
                carry[hh] = (m_new, l, acc)
        for hh, (_, l, acc) in enumerate(carry):
            o_ref[rows, hh * dv:(hh + 1) * dv] = (acc / l).astype(BF16)


def _flash(q, k, v, *, batch, seq, n_heads, nope):
    m = q.shape[0]
    tq = _pick(seq, 512)
    assert tq % CHUNK == 0
    heads = _pick(n_heads, 2)
    dk = 2 * LANE
    blk = lambda b, h: (b, h)
    return pl.pallas_call(
        functools.partial(_flash_kernel, tq=tq, heads=heads, dv=nope),
        grid=(batch, n_heads // heads),
        in_specs=[pl.BlockSpec((seq, heads * dk), blk), pl.BlockSpec((seq, heads * dk), blk),
                  pl.BlockSpec((seq, heads * nope), blk)],
        out_specs=pl.BlockSpec((seq, heads * nope), blk),
        out_shape=jax.ShapeDtypeStruct((m, n_heads * nope), BF16),
        compiler_params=_params(2), name="flash",
    )(q, k, v)


def _cached_attn_kernel(q_ref, cc_ref, kc_ref, cn_ref, kn_ref, wuk_ref, wuv_ref, o_ref,
                        qlat_scr, qr_scr, *, n_heads, nope, rope, seq):
    ckv_c = cc_ref[...].astype(BF16)
    kr_c = kc_ref[...].astype(BF16)
    ckv_n = cn_ref[...].astype(BF16)
    kr_n = kn_ref[...].astype(BF16)
    for hh in range(n_heads):
        c0 = hh * 2 * LANE
        qlat_scr[hh * seq:(hh + 1) * seq, :] = _dot(q_ref[:, c0:c0 + nope], wuk_ref[hh]).astype(BF16)
        qr_scr[hh * seq:(hh + 1) * seq, :] = q_ref[:, c0 + nope:c0 + nope + rope]
    qlat = qlat_scr[...]
    qr = qr_scr[...]
    s_c = _dot_nt(qlat, ckv_c) + _dot_nt(qr, kr_c)
    s_n = _dot_nt(qlat, ckv_n) + _dot_nt(qr, kr_n)
    mx = jnp.maximum(jnp.max(s_c, axis=-1, keepdims=True), jnp.max(s_n, axis=-1, keepdims=True))
    p_c = jnp.exp(s_c - mx)
    p_n = jnp.exp(s_n - mx)
    den = jnp.sum(p_c, axis=-1, keepdims=True) + jnp.sum(p_n, axis=-1, keepdims=True)
    o_lat = ((_dot(p_c.astype(BF16), ckv_c) + _dot(p_n.astype(BF16), ckv_n)) / den).astype(BF16)
    for hh in range(n_heads):
        o_ref[:, hh * nope:(hh + 1) * nope] = _dot(
            o_lat[hh * seq:(hh + 1) * seq, :], wuv_ref[hh]).astype(BF16)


def _cached_attn(q, cache_ckv, cache_kr, ckv_new, kr_new, wuk_h, wuv_h, *, layer, batch, seq,
                 n_heads, nope, rope):
    past, kv_lora = cache_ckv.shape[2], cache_ckv.shape[3]
    assert seq <= CHUNK and past % CHUNK == 0
    hq = q.shape[1]
    return pl.pallas_call(
        functools.partial(_cached_attn_kernel, n_heads=n_heads, nope=nope, rope=rope, seq=seq),
        grid=(batch,),
        in_specs=[pl.BlockSpec((seq, hq), lambda b: (b, 0)),
                  pl.BlockSpec((None, None, past, kv_lora), lambda b: (layer, b, 0, 0)),
                  pl.BlockSpec((None, None, past, rope), lambda b: (layer, b, 0, 0)),
                  pl.BlockSpec((seq, kv_lora), lambda b: (b, 0)),
                  pl.BlockSpec((seq, rope), lambda b: (b, 0)),
                  _const_spec(wuk_h.shape), _const_spec(wuv_h.shape)],
        out_specs=pl.BlockSpec((seq, n_heads * nope), lambda b: (b, 0)),
        out_shape=jax.ShapeDtypeStruct((batch * seq, n_heads * nope), BF16),
        scratch_shapes=[pltpu.VMEM((n_heads * seq, kv_lora), BF16),
                        pltpu.VMEM((n_heads * seq, rope), BF16)],
        compiler_params=_params(1), name="cached_attn",
    )(q, cache_ckv, cache_kr, ckv_new, kr_new, wuk_h, wuv_h)


def _pool_kernel(u_ref, hist_ref, w_ref, scale_ref, y_ref, st_ref, ext, *, tt, pos0):
    ti = pl.program_id(1)
    hp = POOL_HIST_PAD

    @pl.when(ti == 0)
    def _():
        ext[0:hp, :] = hist_ref[...]

    @pl.when(ti > 0)
    def _():
        ext[0:hp, :] = ext[tt:tt + hp, :]

    ext[hp:hp + tt, :] = u_ref[...].astype(F32)
    pos = pos0 + ti * tt + lax.broadcasted_iota(jnp.int32, (tt, 1), 0)
    pg = w_ref.shape[1]
    og = w_ref.shape[2]
    for g, w in enumerate(POOL_WINDOWS):
        c0, c1 = g * pg, (g + 1) * pg
        cur = ext[hp:hp + tt, c0:c1]
        tot = cur
        for k in range(1, w):
            tot = tot + ext[hp - k:hp - k + tt, c0:c1]
        cnt = jnp.minimum(pos + 1, w).astype(F32)
        d = tot / cnt - cur
        y = _dot(d.astype(BF16), w_ref[g]) * scale_ref[:, g * og:(g + 1) * og]
        y_ref[:, g * og:(g + 1) * og] = y.astype(BF16)
    st_ref[...] = ext[tt:tt + hp, :]


def _pool(pa, hist, w_pool, scale, *, batch, seq, col0, pos0):
    m = pa.shape[0]
    g, pg, og = w_pool.shape
    width = g * pg
    assert col0 % width == 0
    tt = _pick(seq, 256)
    assert tt >= POOL_HIST_PAD
    nt = seq // tt
    return pl.pallas_call(
        functools.partial(_pool_kernel, tt=tt, pos0=pos0),
        grid=(batch, nt),
        in_specs=[pl.BlockSpec((tt, width), lambda b, t: (b * nt + t, col0 // width)),
                  pl.BlockSpec((None, POOL_HIST_PAD, width), lambda b, t: (b, 0, 0)),
                  _const_spec(w_pool.shape), _const_spec(scale.shape)],
        out_specs=[pl.BlockSpec((tt, g * og), lambda b, t: (b * nt + t, 0)),
                   pl.BlockSpec((None, POOL_HIST_PAD, width), lambda b, t: (b, 0, 0))],
        out_shape=[jax.ShapeDtypeStruct((m, g * og), BF16),
                   jax.ShapeDtypeStruct((batch, POOL_HIST_PAD, width), F32)],
        scratch_shapes=[pltpu.VMEM((POOL_HIST_PAD + tt, width), F32)],
        compiler_params=_params(2), name="pool",
    )(pa, hist, w_pool, scale)


def _conv_kernel(a_ref, gate_ref, hist_ref, wdw_ref, bdw_ref, lng_ref, lnb_ref, wpw_ref,
                 y_ref, st_ref, ext, *, tt):
    ti = pl.program_id(1)
    hp = CONV_HIST_PAD

    @pl.when(ti == 0)
    def _():
        ext[0:hp, :] = hist_ref[...]

    @pl.when(ti > 0)
    def _():
        ext[0:hp, :] = ext[tt:tt + hp, :]

    ext[hp:hp + tt, :] = a_ref[...].astype(F32) * jax.nn.sigmoid(gate_ref[...].astype(F32))
    off = hp - CONV_HIST
    acc = bdw_ref[...] + wdw_ref[0:1, :] * ext[off:off + tt, :]
    for k in range(1, CONV_K):
        acc = acc + wdw_ref[k:k + 1, :] * ext[off + k:off + k + tt, :]
    mu = jnp.mean(acc, axis=-1, keepdims=True)
    cen = acc - mu
    var = jnp.mean(cen * cen, axis=-1, keepdims=True)
    y = cen * lax.rsqrt(var + EPS) * lng_ref[...] + lnb_ref[...]
    y = y * jax.nn.sigmoid(y)
    y_ref[...] = _dot(y.astype(BF16), wpw_ref[...]).astype(BF16)
    st_ref[...] = ext[tt:tt + hp, :]


def _conv(pa, hist, wdw, bdw, lng, lnb, wpw, *, batch, seq, col0):
    m = pa.shape[0]
    width, d = wpw.shape
    assert col0 % width == 0
    tt = _pick(seq, 256)
    assert tt >= CONV_HIST_PAD
    nt = seq // tt
    cb = col0 // width
    return pl.pallas_call(
        functools.partial(_conv_kernel, tt=tt),
        grid=(batch, nt),
        in_specs=[pl.BlockSpec((tt, width), lambda b, t: (b * nt + t, cb)),
                  pl.BlockSpec((tt, width), lambda b, t: (b * nt + t, cb + 1)),
                  pl.BlockSpec((None, CONV_HIST_PAD, width), lambda b, t: (b, 0, 0)),
                  _const_spec(wdw.shape), _const_spec(bdw.shape), _const_spec(lng.shape),
                  _const_spec(lnb.shape), _const_spec(wpw.shape)],
        out_specs=[pl.BlockSpec((tt, d), lambda b, t: (b * nt + t, 0)),
                   pl.BlockSpec((None, CONV_HIST_PAD, width), lambda b, t: (b, 0, 0))],
        out_shape=[jax.ShapeDtypeStruct((m, d), BF16),
                   jax.ShapeDtypeStruct((batch, CONV_HIST_PAD, width), F32)],
        scratch_shapes=[pltpu.VMEM((CONV_HIST_PAD + tt, width), F32)],
        compiler_params=_params(2), name="conv",
    )(pa, pa, hist, wdw, bdw, lng, lnb, wpw)


def _merge_kernel(g0_ref, g1_ref, g2_ref, bg_ref, yp_ref, yc_ref, o_ref, wo_ref, wout_ref, x_ref,
                  gpost_ref, gnext_ref, xo_ref, h_ref):
    d = x_ref.shape[1]
    bg = bg_ref[...]
    gate0 = jax.nn.sigmoid(g0_ref[...].astype(F32) + bg[:, 0:d])
    gate1 = jax.nn.sigmoid(g1_ref[...].astype(F32) + bg[:, d:2 * d])
    gate2 = jax.nn.sigmoid(g2_ref[...].astype(F32) + bg[:, 2 * d:3 * d])
    y_mla = _dot(o_ref[...], wo_ref[...])
    merged = gate0 * yp_ref[...].astype(F32) + gate1 * yc_ref[...].astype(F32) + gate2 * y_mla
    z = _dot(merged.astype(BF16), wout_ref[...])
    xn = x_ref[...] + _rms(z, gpost_ref[...])
    xo_ref[...] = xn
    h_ref[...] = _rms(xn, gnext_ref[...]).astype(BF16)


def _merge(pa, bg, yp, yc, o, wo, wout, x, gpost, gnext):
    m, d = x.shape
    tm = _pick(m, 256)
    row = lambda i: (i, 0)
    return pl.pallas_call(
        _merge_kernel,
        grid=(m // tm,),
        in_specs=[pl.BlockSpec((tm, d), lambda i: (i, 0)), pl.BlockSpec((tm, d), lambda i: (i, 1)),
                  pl.BlockSpec((tm, d), lambda i: (i, 2)), _const_spec(bg.shape),
                  pl.BlockSpec((tm, d), row), pl.BlockSpec((tm, d), row),
                  pl.BlockSpec((tm, o.shape[1]), row), _const_spec(wo.shape),
                  _const_spec(wout.shape), pl.BlockSpec((tm, d), row),
                  _const_spec(gpost.shape), _const_spec(gnext.shape)],
        out_specs=[pl.BlockSpec((tm, d), row), pl.BlockSpec((tm, d), row)],
        out_shape=[jax.ShapeDtypeStruct((m, d), F32), jax.ShapeDtypeStruct((m, d), BF16)],
        compiler_params=_params(1), name="merge",
    )(pa, pa, pa, bg, yp, yc, o, wo, wout, x, gpost, gnext)


def _ffn_kernel(*refs, seg, nseg, nt, nk, emit_h):
    (h_ref, hg_ref, hv_ref, wug_ref, wuv_ref, wg_ref, wv_ref, bgc_ref, bvc_ref, wd_ref, x_ref,
     gpost_ref) = refs[:12]
    rest = refs[12:]
    if emit_h:
        gnext_ref, xo_ref, hn_ref, sg_ref, sv_ref, carry = rest
    else:
        xo_ref, sg_ref, sv_ref, carry = rest
    i = pl.program_id(0)
    kk = pl.program_id(1)
    hp = FFN_HIST_PAD
    rc = min(seg, FFN_ROW_CHUNK)
    per_seg = seg // rc

    @pl.when(kk == 0)
    def _():
        xo_ref[...] = jnp.zeros_like(xo_ref)

    def conv3(prev, cur, w_ref, b_ref):
        ext = jnp.concatenate([prev, cur], axis=0)
        y = b_ref[...] + w_ref[FFN_K - 1:FFN_K, :] * cur
        for k in range(FFN_K - 1):
            o = hp - FFN_HIST + k
            y = y + w_ref[k:k + 1, :] * ext[o:o + rc, :]
        return y

    for c in range(nseg * per_seg):
        s = c // per_seg
        rows = slice(c * rc, (c + 1) * rc)
        hc = h_ref[rows, :]
        ug = _dot(hc, wug_ref[...])
        uv = _dot(hc, wuv_ref[...])
        if c % per_seg == 0:
            prev_g, prev_v = hg_ref[s], hv_ref[s]
            if nt > 1:
                first = (i % nt) == 0
                prev_g = jnp.where(first, prev_g, carry[kk, 0])
                prev_v = jnp.where(first, prev_v, carry[kk, 1])
        f = (jax.nn.gelu(conv3(prev_g, ug, wg_ref, bgc_ref), approximate=True)
             * conv3(prev_v, uv, wv_ref, bvc_ref))
        xo_ref[rows, :] += _dot(f.astype(BF16), wd_ref[...])
        prev_g, prev_v = ug[rc - hp:, :], uv[rc - hp:, :]
        if c % per_seg == per_seg - 1:
            sg_ref[s] = prev_g
            sv_ref[s] = prev_v
            if nt > 1:
                carry[kk, 0] = prev_g
                carry[kk, 1] = prev_v

    @pl.when(kk == nk - 1)
    def _():
        xn = x_ref[...] + _rms(xo_ref[...], gpost_ref[...])
        xo_ref[...] = xn
        if emit_h:
            hn_ref[...] = _rms(xn, gnext_ref[...]).astype(BF16)


def _ffn(h, hist_g, hist_v, wup, wdw, bdw, wd, x, gpost, gnext, *, batch, seq):
    m, d = x.shape
    f = wd.shape[0]
    emit_h = gnext is not None
    tm = _pick(m, 1024)
    if tm >= seq:
        assert tm % seq == 0
        seg, nseg, nt = seq, tm // seq, 1
    else:
        assert seq % tm == 0
        seg, nseg, nt = tm, 1, seq // tm
    assert seg >= FFN_HIST_PAD
    tk = _pick(f, 256)
    nk = f // tk
    row = lambda i, k: (i, 0)
    one = pl.Buffered(1)
    hist_spec = pl.BlockSpec((nseg, FFN_HIST_PAD, tk), lambda i, k: (i // nt, 0, k))
    tail_spec = pl.BlockSpec((nseg, FFN_HIST_PAD, tk), lambda i, k: (i, 0, k))
    in_specs = [pl.BlockSpec((tm, d), row), hist_spec, hist_spec,
                pl.BlockSpec((d, tk), lambda i, k: (0, k)),
                pl.BlockSpec((d, tk), lambda i, k: (0, k + nk)),
                pl.BlockSpec((FFN_HIST_PAD, tk), lambda i, k: (0, k)),
                pl.BlockSpec((FFN_HIST_PAD, tk), lambda i, k: (0, k + nk)),
                pl.BlockSpec((1, tk), lambda i, k: (0, k)),
                pl.BlockSpec((1, tk), lambda i, k: (0, k + nk)),
                pl.BlockSpec((tk, d), lambda i, k: (k, 0)),
                pl.BlockSpec((tm, d), row, pipeline_mode=one), _const_spec(gpost.shape)]
    args = [h, hist_g, hist_v, wup, wup, wdw, wdw, bdw, bdw, wd, x, gpost]
    out_specs = [pl.BlockSpec((tm, d), row, pipeline_mode=one)]
    out_shape = [jax.ShapeDtypeStruct((m, d), F32)]
    if emit_h:
        in_specs.append(_const_spec(gnext.shape))
        args.append(gnext)
        out_specs.append(pl.BlockSpec((tm, d), row, pipeline_mode=one))
        out_shape.append(jax.ShapeDtypeStruct((m, d), BF16))
    out_specs += [tail_spec, tail_spec]
    out_shape += [jax.ShapeDtypeStruct((m // seg, FFN_HIST_PAD, f), F32)] * 2
    return pl.pallas_call(
        functools.partial(_ffn_kernel, seg=seg, nseg=nseg, nt=nt, nk=nk, emit_h=emit_h),
        grid=(m // tm, nk), in_specs=in_specs, out_specs=out_specs, out_shape=out_shape,
        scratch_shapes=[pltpu.VMEM((nk, 2, FFN_HIST_PAD, tk), F32)],
        compiler_params=_params(2), name="ffn",
    )(*args)


def _rotate_half_cols(w):
    half = w.shape[-1] // 2
    return jnp.concatenate([-w[..., half:], w[..., :half]], axis=-1)


def _pad_last(w, n):
    return jnp.pad(w, [(0, 0)] * (w.ndim - 1) + [(0, n - w.shape[-1])])


def _prep_layer(l, dims, w_in, w_uq, w_uk, w_uv, w_o_mla, w_pool, w_conv_dw, w_conv_pw, w_out,
                w_up, w_ffn_dw, w_down):
    d, pw, cw, ql, kl, rope, nope, nh = dims
    o_glu = pw
    o_q = o_glu + 2 * cw
    o_kv = o_q + ql
    o_kr = o_kv + kl
    o_gate = o_kr + rope
    wi = w_in[l]
    wa = jnp.concatenate([wi[:, o_gate:], wi[:, o_glu:o_q], wi[:, :o_glu]], axis=1).astype(BF16)
    wkr = wi[:, o_kr:o_gate]
    wb = jnp.concatenate([wi[:, o_q:o_kr], _pad_last(wkr, LANE),
                          _pad_last(_rotate_half_cols(wkr), LANE)], axis=1).astype(BF16)
    uq = w_uq[l]
    uq_r = uq[..., nope:]
    wq = jnp.concatenate([uq[..., :nope].reshape(ql, nh * nope),
                          _pad_last(uq_r, LANE).reshape(ql, nh * LANE),
                          _pad_last(_rotate_half_cols(uq_r), LANE).reshape(ql, nh * LANE)],
                         axis=1).astype(BF16)
    uk, uv = w_uk[l], w_uv[l]
    return dict(
        wa=wa, wb=wb, wq=wq,
        wuk=uk.reshape(kl, -1).astype(BF16), wuv=uv.reshape(kl, -1).astype(BF16),
        wuk_h=jnp.transpose(uk, (1, 2, 0)).astype(BF16), wuv_h=jnp.transpose(uv, (1, 0, 2)).astype(BF16),
        wo=w_o_mla[l].astype(BF16), wpool=w_pool[l].astype(BF16),
        wdw=_pad_rows(w_conv_dw[l], CONV_HIST_PAD), wpw=w_conv_pw[l].astype(BF16),
        wout=w_out[l].astype(BF16), wup=w_up[l].astype(BF16),
        wfdw=_pad_rows(w_ffn_dw[l], FFN_HIST_PAD), wdown=w_down[l].astype(BF16))


def _pad_rows(w, n):
    return jnp.pad(w, [(0, n - w.shape[0])] + [(0, 0)] * (w.ndim - 1))


def _pad_front(s, n):
    return jnp.pad(s, [(0, 0), (n - s.shape[1], 0), (0, 0)])


def _rope_tables(pos0, seq, rope):
    half = rope // 2
    inv = ROPE_THETA ** (-jnp.arange(half, dtype=F32) / half)
    pos = pos0 + jnp.arange(seq, dtype=jnp.int32)
    ang = pos.astype(F32)[:, None] * inv[None, :]
    cos, sin = jnp.cos(ang), jnp.sin(ang)
    return (_pad_last(jnp.concatenate([cos, cos], axis=1), LANE),
            _pad_last(jnp.concatenate([sin, sin], axis=1), LANE))


def _trunk(x3, pos0, cache_ckv, cache_kr, state_pool, state_conv, state_ffn, layers, vecs, dims):
    batch, seq, d = x3.shape
    _, pw, cw, ql, kl, rope, nope, nh = dims
    depth = len(layers)
    m = batch * seq
    x = x3.reshape(m, d)
    cached = cache_ckv is not None
    scale = 1.0 / math.sqrt(nope + rope)
    if not cached:
        scale *= math.log2(math.e)
    cos_t, sin_t = _rope_tables(pos0, seq, rope)
    col_glu = 3 * d
    col_pool = 3 * d + 2 * cw
    row = lambda v: v.reshape(1, -1)
    h = _norm_cast(x, row(vecs["g_pre_mix"][0]))
    outs = {k: [] for k in ("ckv", "kr", "pool", "conv", "ffn")}
    for l in range(depth):
        w = layers[l]
        pa = _matmul(h, w["wa"])
        res = _mla_prep(h, w["wb"], row(vecs["g_q_a"][l]), row(vecs["g_kv_a"][l]), cos_t, sin_t,
                        w["wq"], w["wuk"], w["wuv"], seq=seq, n_heads=nh, nope=nope, rope=rope,
                        scale=scale, expand_kv=not cached)
        if cached:
            ckv, kr, q = res
            o = _cached_attn(q, cache_ckv, cache_kr, ckv, kr, w["wuk_h"], w["wuv_h"], layer=l,
                             batch=batch, seq=seq, n_heads=nh, nope=nope, rope=rope)
            hist_pool = _pad_front(state_pool[l], POOL_HIST_PAD)
            hist_conv = _pad_front(state_conv[l], CONV_HIST_PAD)
            hist_ffn = _pad_front(state_ffn[l], FFN_HIST_PAD)
        else:
            ckv, kr, q, kfull, vfull = res
            o = _flash(q, kfull, vfull, batch=batch, seq=seq, n_heads=nh, nope=nope)
            hist_pool = jnp.zeros((batch, POOL_HIST_PAD, pw), F32)
            hist_conv = jnp.zeros((batch, CONV_HIST_PAD, cw), F32)
            hist_ffn = jnp.zeros((batch, FFN_HIST_PAD, w["wfdw"].shape[1]), F32)
        yp, st_pool = _pool(pa, hist_pool, w["wpool"], row(vecs["pool_scale"][l]), batch=batch,
                            seq=seq, col0=col_pool, pos0=pos0)
        yc, st_conv = _conv(pa, hist_conv, w["wdw"], row(vecs["b_conv_dw"][l]),
                            row(vecs["g_conv_ln"][l]), row(vecs["b_conv_ln"][l]), w["wpw"],
                            batch=batch, seq=seq, col0=col_glu)
        x, h2 = _merge(pa, row(vecs["b_gate"][l]), yp, yc, o, w["wo"], w["wout"], x,
                       row(vecs["g_post_mix"][l]), row(vecs["g_pre_ffn"][l]))
        f = w["wdown"].shape[0]
        g_next = row(vecs["g_pre_mix"][l + 1]) if l + 1 < depth else None
        res = _ffn(h2, hist_ffn[..., :f], hist_ffn[..., f:], w["wup"], w["wfdw"],
                   row(vecs["b_ffn_dw"][l]), w["wdown"], x, row(vecs["g_post_ffn"][l]), g_next,
                   batch=batch, seq=seq)
        if g_next is None:
            x, st_g, st_v = res
        else:
            x, h, st_g, st_v = res
        outs["ckv"].append(ckv.reshape(batch, seq, kl))
        outs["kr"].append(kr.reshape(batch, seq, rope))
        outs["pool"].append(st_pool[:, POOL_HIST_PAD - POOL_HIST:])
        outs["conv"].append(st_conv[:, CONV_HIST_PAD - CONV_HIST:])
        tails = jnp.concatenate([st_g, st_v], axis=-1)
        tails = tails.reshape(batch, -1, FFN_HIST_PAD, 2 * f)[:, -1]
        outs["ffn"].append(tails[:, FFN_HIST_PAD - FFN_HIST:])
    return (x.reshape(batch, seq, d),) + tuple(jnp.stack(outs[k]) for k in ("ckv", "kr", "pool", "conv", "ffn"))


def kernel(x_prompt, x_sample, cache_ckv, cache_krope, state_pool, state_conv, state_ffn,
           g_pre_mix, w_in, b_gate, g_q_a, g_kv_a, w_uq, w_uk, w_uv, w_o_mla,
           w_pool, pool_scale, w_conv_dw, b_conv_dw, g_conv_ln, b_conv_ln, w_conv_pw,
           w_out, g_post_mix, g_pre_ffn, w_up, w_ffn_dw, b_ffn_dw, w_down, g_post_ffn):
    depth, d = g_pre_mix.shape
    nh = w_uq.shape[2]
    nope = w_uk.shape[3]
    rope = w_uq.shape[3] - nope
    dims = (d, state_pool.shape[3], state_conv.shape[3], w_uq.shape[1], w_uk.shape[1], rope, nope, nh)
    layers = [_prep_layer(l, dims, w_in, w_uq, w_uk, w_uv, w_o_mla, w_pool, w_conv_dw, w_conv_pw,
                          w_out, w_up, w_ffn_dw, w_down) for l in range(depth)]
    vecs = dict(g_pre_mix=g_pre_mix, b_gate=b_gate, g_q_a=g_q_a, g_kv_a=g_kv_a,
                pool_scale=pool_scale, b_conv_dw=b_conv_dw, g_conv_ln=g_conv_ln,
                b_conv_ln=b_conv_ln, g_post_mix=g_post_mix, g_pre_ffn=g_pre_ffn,
                b_ffn_dw=b_ffn_dw, g_post_ffn=g_post_ffn)
    prompt = _trunk(x_prompt, 0, None, None, None, None, None, layers, vecs, dims)
    sample = _trunk(x_sample, cache_ckv.shape[2], cache_ckv, cache_krope, state_pool, state_conv,
                    state_ffn, layers, vecs, dims)
    return (prompt[0], sample[0]) + prompt[1:] + sample[1:]
```

```python
import functools
import math

import jax
import jax.numpy as jnp
from jax import lax
from jax.experimental import pallas as pl
from jax.experimental.pallas import tpu as pltpu

CHUNK = 64
EPS = 1e-6
NEG_INF = -1e30
POOL_WINDOWS = (2, 4, 8, 16)
POOL_HIST = max(POOL_WINDOWS) - 1
POOL_HIST_PAD = 16
CONV_K = 31
CONV_HIST = CONV_K - 1
CONV_HIST_PAD = 32
CONV_SUBLANES = 8
CONV_ROW_BLOCK = 32
FFN_K = 3
FFN_HIST = FFN_K - 1
FFN_HIST_PAD = 8
FFN_ROW_CHUNK = 512
ROPE_THETA = 10000.0
LANE = 128
V7X_VMEM_LIMIT = 60 * 1024 * 1024

F32 = jnp.float32
BF16 = jnp.bfloat16


def _params(n_axes):
    return pltpu.CompilerParams(dimension_semantics=("arbitrary",) * n_axes,
                                vmem_limit_bytes=V7X_VMEM_LIMIT)


def _const_spec(shape):
    zeros = (0,) * len(shape)
    return pl.BlockSpec(shape, lambda *_: zeros, pipeline_mode=pl.Buffered(1))


def _dot(a, b):
    return jnp.dot(a, b, preferred_element_type=F32)


def _dot_nt(a, b):
    return lax.dot_general(a, b, (((1,), (1,)), ((), ())), preferred_element_type=F32)


def _rms(x, g):
    return x * lax.rsqrt(jnp.mean(x * x, axis=-1, keepdims=True) + EPS) * g


def _pick(n, pref):
    t = min(n, pref)
    while n % t:
        t -= 1
    return t


def _norm_kernel(x_ref, g_ref, h_ref):
    h_ref[...] = _rms(x_ref[...], g_ref[...]).astype(BF16)


def _norm_cast(x, g):
    m, d = x.shape
    tm = _pick(m, 512)
    return pl.pallas_call(
        _norm_kernel,
        grid=(m // tm,),
        in_specs=[pl.BlockSpec((tm, d), lambda i: (i, 0)), _const_spec((1, d))],
        out_specs=pl.BlockSpec((tm, d), lambda i: (i, 0)),
        out_shape=jax.ShapeDtypeStruct((m, d), BF16),
        compiler_params=_params(1),
        name="norm_cast",
    )(x, g)


def _mm_kernel(x_ref, w_ref, o_ref):
    o_ref[...] = _dot(x_ref[...], w_ref[...]).astype(o_ref.dtype)


def _matmul(x, w, tm_pref=1024, tn_pref=1024):
    m, k = x.shape
    n = w.shape[1]
    tm = _pick(m, tm_pref)
    tn = _pick(n, tn_pref)
    return pl.pallas_call(
        _mm_kernel,
        grid=(m // tm, n // tn),
        in_specs=[pl.BlockSpec((tm, k), lambda i, j: (i, 0)),
                  pl.BlockSpec((k, tn), lambda i, j: (0, j))],
        out_specs=pl.BlockSpec((tm, tn), lambda i, j: (i, j)),
        out_shape=jax.ShapeDtypeStruct((m, n), BF16),
        compiler_params=_params(2),
        name="matmul",
    )(x, w)


def _mla_prep_kernel(h_ref, wb_ref, gq_ref, gkv_ref, cos_ref, sin_ref, wq_ref, *rest,
                     n_heads, q_lora, kv_lora, nope, rope, scale, expand_kv):
    if expand_kv:
        wuk_ref, wuv_ref, ckv_ref, kr_ref, q_ref, k_ref, v_ref = rest
    else:
        ckv_ref, kr_ref, q_ref = rest
    hn = n_heads * nope
    pb = _dot(h_ref[...], wb_ref[...])
    cq = _rms(pb[:, :q_lora], gq_ref[...])
    ckv = _rms(pb[:, q_lora:q_lora + kv_lora], gkv_ref[...])
    cos = cos_ref[...]
    sin = sin_ref[...]
    o = q_lora + kv_lora
    kr = pb[:, o:o + LANE] * cos + pb[:, o + LANE:o + 2 * LANE] * sin
    ckv_ref[...] = ckv
    kr_ref[...] = kr[:, :rope]
    qall = _dot(cq.astype(BF16), wq_ref[...])
    for hh in range(n_heads):
        c0 = hh * LANE
        qn = qall[:, c0:c0 + nope] * scale
        qr = (qall[:, hn + c0:hn + c0 + LANE] * cos
              + qall[:, 2 * hn + c0:2 * hn + c0 + LANE] * sin) * scale
        q_ref[:, 2 * c0:2 * c0 + nope] = qn.astype(BF16)
        q_ref[:, 2 * c0 + nope:2 * c0 + nope + LANE] = qr.astype(BF16)
    if expand_kv:
        ckvb = ckv.astype(BF16)
        kn = _dot(ckvb, wuk_ref[...])
        v_ref[...] = _dot(ckvb, wuv_ref[...]).astype(BF16)
        krb = kr.astype(BF16)
        for hh in range(n_heads):
            c0 = hh * LANE
            k_ref[:, 2 * c0:2 * c0 + nope] = kn[:, c0:c0 + nope].astype(BF16)
            k_ref[:, 2 * c0 + nope:2 * c0 + nope + LANE] = krb


def _mla_prep(h, wb, gq, gkv, cos_t, sin_t, wq, wuk, wuv, *, seq, n_heads, nope, rope, scale,
              expand_kv):
    m, d = h.shape
    q_lora, kv_lora = gq.shape[1], gkv.shape[1]
    assert nope == LANE and rope <= LANE
    tm = _pick(seq, 256)
    nt = seq // tm
    hq = n_heads * (nope + LANE)
    row = lambda i: (i, 0)
    tab = lambda i: (i % nt, 0)
    in_specs = [pl.BlockSpec((tm, d), row), _const_spec(wb.shape), _const_spec(gq.shape),
                _const_spec(gkv.shape), pl.BlockSpec((tm, LANE), tab), pl.BlockSpec((tm, LANE), tab),
                _const_spec(wq.shape)]
    args = [h, wb, gq, gkv, cos_t, sin_t, wq]
    out_specs = [pl.BlockSpec((tm, kv_lora), row), pl.BlockSpec((tm, rope), row),
                 pl.BlockSpec((tm, hq), row)]
    out_shape = [jax.ShapeDtypeStruct((m, kv_lora), F32), jax.ShapeDtypeStruct((m, rope), F32),
                 jax.ShapeDtypeStruct((m, hq), BF16)]
    if expand_kv:
        in_specs += [_const_spec(wuk.shape), _const_spec(wuv.shape)]
        args += [wuk, wuv]
        out_specs += [pl.BlockSpec((tm, hq), row), pl.BlockSpec((tm, n_heads * nope), row)]
        out_shape += [jax.ShapeDtypeStruct((m, hq), BF16),
                      jax.ShapeDtypeStruct((m, n_heads * nope), BF16)]
    return pl.pallas_call(
        functools.partial(_mla_prep_kernel, n_heads=n_heads, q_lora=q_lora, kv_lora=kv_lora,
                          nope=nope, rope=rope, scale=scale, expand_kv=expand_kv),
        grid=(m // tm,), in_specs=in_specs, out_specs=out_specs, out_shape=out_shape,
        compiler_params=_params(1), name="mla_prep",
    )(*args)


def _flash_kernel(q_ref, k_ref, v_ref, o_ref, *, tq, heads, dv):
    dk = 2 * LANE
    nq = q_ref.shape[0] // tq
    r = lax.broadcasted_iota(jnp.int32, (tq, tq), 0) // CHUNK
    c = lax.broadcasted_iota(jnp.int32, (tq, tq), 1) // CHUNK
    visible = c <= r
    for qi in range(nq):
        rows = slice(qi * tq, (qi + 1) * tq)
        carry = [(jnp.full((tq, 1), NEG_INF, F32), jnp.zeros((tq, 1), F32),
                  jnp.zeros((tq, dv), F32)) for _ in range(heads)]
        for j in range(qi + 1):
            keys = slice(j * tq, (j + 1) * tq)
            for hh in range(heads):
                m, l, acc = carry[hh]
                s = _dot_nt(q_ref[rows, hh * dk:(hh + 1) * dk],
                            k_ref[keys, hh * dk:(hh + 1) * dk])
                if j == qi:
                    s = jnp.where(visible, s, NEG_INF)
                m_new = jnp.maximum(m, jnp.max(s, axis=-1, keepdims=True))
                p = jnp.exp2(s - m_new)
                alpha = jnp.exp2(m - m_new)
                l = alpha * l + jnp.sum(p, axis=-1, keepdims=True)
                acc = alpha * acc + _dot(p.astype(BF16), v_ref[keys, hh * dv:(hh + 1) * dv])
                carry[hh] = (m_new, l, acc)
        for hh, (_, l, acc) in enumerate(carry):
            o_ref[rows, hh * dv:(hh + 1) * dv] = (acc / l).astype(BF16)


def _flash(q, k, v, *, batch, seq, n_heads, nope):
    m = q.shape[0]
    tq = _pick(seq, 512)
    assert tq % CHUNK == 0
    heads = _pick(n_heads, 2)
    dk = 2 * LANE
    blk = lambda b, h: (b, h)
    return pl.pallas_call(
        functools.partial(_flash_kernel, tq=tq, heads=heads, dv=nope),
        grid=(batch, n_heads // heads),
        in_specs=[pl.BlockSpec((seq, heads * dk), blk), pl.BlockSpec((seq, heads * dk), blk),
                  pl.BlockSpec((seq, heads * nope), blk)],
        out_specs=pl.BlockSpec((seq, heads * nope), blk),
        out_shape=jax.ShapeDtypeStruct((m, n_heads * nope), BF16),
        compiler_params=_params(2), name="flash",
    )(q, k, v)


def _cached_attn_kernel(q_ref, cc_ref, kc_ref, cn_ref, kn_ref, wuk_ref, wuv_ref, o_ref,
                        qlat_scr, qr_scr, *, n_heads, nope, rope, seq):
    ckv_c = cc_ref[...].astype(BF16)
    kr_c = kc_ref[...].astype(BF16)
    ckv_n = cn_ref[...].astype(BF16)
    kr_n = kn_ref[...].astype(BF16)
    for hh in range(n_heads):
        c0 = hh * 2 * LANE
        qlat_scr[hh * seq:(hh + 1) * seq, :] = _dot(q_ref[:, c0:c0 + nope], wuk_ref[hh]).astype(BF16)
        qr_scr[hh * seq:(hh + 1) * seq, :] = q_ref[:, c0 + nope:c0 + nope + rope]
    qlat = qlat_scr[...]
    qr = qr_scr[...]
    s_c = _dot_nt(qlat, ckv_c) + _dot_nt(qr, kr_c)
    s_n = _dot_nt(qlat, ckv_n) + _dot_nt(qr, kr_n)
    mx = jnp.maximum(jnp.max(s_c, axis=-1, keepdims=True), jnp.max(s_n, axis=-1, keepdims=True))
    p_c = jnp.exp(s_c - mx)
    p_n = jnp.exp(s_n - mx)
    den = jnp.sum(p_c, axis=-1, keepdims=True) + jnp.sum(p_n, axis=-1, keepdims=True)
    o_lat = ((_dot(p_c.astype(BF16), ckv_c) + _dot(p_n.astype(BF16), ckv_n)) / den).astype(BF16)
    for hh in range(n_heads):
        o_ref[:, hh * nope:(hh + 1) * nope] = _dot(
            o_lat[hh * seq:(hh + 1) * seq, :], wuv_ref[hh]).astype(BF16)


def _cached_attn(q, cache_ckv, cache_kr, ckv_new, kr_new, wuk_h, wuv_h, *, layer, batch, seq,
                 n_heads, nope, rope):
    past, kv_lora = cache_ckv.shape[2], cache_ckv.shape[3]
    assert seq <= CHUNK and past % CHUNK == 0
    hq = q.shape[1]
    return pl.pallas_call(
        functools.partial(_cached_attn_kernel, n_heads=n_heads, nope=nope, rope=rope, seq=seq),
        grid=(batch,),
        in_specs=[pl.BlockSpec((seq, hq), lambda b: (b, 0)),
                  pl.BlockSpec((None, None, past, kv_lora), lambda b: (layer, b, 0, 0)),
                  pl.BlockSpec((None, None, past, rope), lambda b: (layer, b, 0, 0)),
                  pl.BlockSpec((seq, kv_lora), lambda b: (b, 0)),
                  pl.BlockSpec((seq, rope), lambda b: (b, 0)),
                  _const_spec(wuk_h.shape), _const_spec(wuv_h.shape)],
        out_specs=pl.BlockSpec((seq, n_heads * nope), lambda b: (b, 0)),
        out_shape=jax.ShapeDtypeStruct((batch * seq, n_heads * nope), BF16),
        scratch_shapes=[pltpu.VMEM((n_heads * seq, kv_lora), BF16),
                        pltpu.VMEM((n_heads * seq, rope), BF16)],
        compiler_params=_params(1), name="cached_attn",
    )(q, cache_ckv, cache_kr, ckv_new, kr_new, wuk_h, wuv_h)


def _pool_kernel(u_ref, hist_ref, w_ref, scale_ref, y_ref, st_ref, ext, *, tt, pos0):
    ti = pl.program_id(1)
    hp = POOL_HIST_PAD

    @pl.when(ti == 0)
    def _():
        ext[0:hp, :] = hist_ref[...]

    @pl.when(ti > 0)
    def _():
        ext[0:hp, :] = ext[tt:tt + hp, :]

    ext[hp:hp + tt, :] = u_ref[...].astype(F32)
    pos = pos0 + ti * tt + lax.broadcasted_iota(jnp.int32, (tt, 1), 0)
    pg = w_ref.shape[1]
    og = w_ref.shape[2]
    for g, w in enumerate(POOL_WINDOWS):
        c0, c1 = g * pg, (g + 1) * pg
        cur = ext[hp:hp + tt, c0:c1]
        tot = cur
        for k in range(1, w):
            tot = tot + ext[hp - k:hp - k + tt, c0:c1]
        cnt = jnp.minimum(pos + 1, w).astype(F32)
        d = tot / cnt - cur
        y = _dot(d.astype(BF16), w_ref[g]) * scale_ref[:, g * og:(g + 1) * og]
        y_ref[:, g * og:(g + 1) * og] = y.astype(BF16)
    st_ref[...] = ext[tt:tt + hp, :]


def _pool(pa, hist, w_pool, scale, *, batch, seq, col0, pos0):
    m = pa.shape[0]
    g, pg, og = w_pool.shape
    width = g * pg
    assert col0 % width == 0
    tt = _pick(seq, 256)
    assert tt >= POOL_HIST_PAD
    nt = seq // tt
    return pl.pallas_call(
        functools.partial(_pool_kernel, tt=tt, pos0=pos0),
        grid=(batch, nt),
        in_specs=[pl.BlockSpec((tt, width), lambda b, t: (b * nt + t, col0 // width)),
                  pl.BlockSpec((None, POOL_HIST_PAD, width), lambda b, t: (b, 0, 0)),
                  _const_spec(w_pool.shape), _const_spec(scale.shape)],
        out_specs=[pl.BlockSpec((tt, g * og), lambda b, t: (b * nt + t, 0)),
                   pl.BlockSpec((None, POOL_HIST_PAD, width), lambda b, t: (b, 0, 0))],
        out_shape=[jax.ShapeDtypeStruct((m, g * og), BF16),
                   jax.ShapeDtypeStruct((batch, POOL_HIST_PAD, width), F32)],
        scratch_shapes=[pltpu.VMEM((POOL_HIST_PAD + tt, width), F32)],
        compiler_params=_params(2), name="pool",
    )(pa, hist, w_pool, scale)


def _conv_kernel(a_ref, gate_ref, hist_ref, wdw_ref, bdw_ref, lng_ref, lnb_ref, wpw_ref,
                 y_ref, st_ref, ext, shifted, dw_out, *, tt):
    ti = pl.program_id(1)
    hp = CONV_HIST_PAD
    sub = CONV_SUBLANES

    @pl.when(ti == 0)
    def _():
        ext[0:hp, :] = hist_ref[...]

    @pl.when(ti > 0)
    def _():
        ext[0:hp, :] = ext[tt:tt + hp, :]

    ext[hp:hp + tt, :] = a_ref[...].astype(F32) * jax.nn.sigmoid(gate_ref[...].astype(F32))
    off = hp - CONV_HIST
    for r in range(1, sub):
        n = tt + sub * ((off + CONV_K - 1 - r) // sub)
        shifted[r - 1, 0:n, :] = ext[r:r + n, :]

    def block(bi, _):
        r0 = pl.multiple_of(bi * CONV_ROW_BLOCK, CONV_ROW_BLOCK)
        acc = jnp.broadcast_to(bdw_ref[...], (CONV_ROW_BLOCK, ext.shape[1]))
        for k in range(CONV_K):
            a, r = divmod(off + k, sub)
            rows = pl.ds(r0 + sub * a, CONV_ROW_BLOCK)
            src = ext[rows, :] if r == 0 else shifted[r - 1, rows, :]
            acc = acc + wdw_ref[k:k + 1, :] * src
        dw_out[pl.ds(r0, CONV_ROW_BLOCK), :] = acc
        return 0

    lax.fori_loop(0, tt // CONV_ROW_BLOCK, block, 0)
    acc = dw_out[...]
    mu = jnp.mean(acc, axis=-1, keepdims=True)
    cen = acc - mu
    var = jnp.mean(cen * cen, axis=-1, keepdims=True)
    y = cen * lax.rsqrt(var + EPS) * lng_ref[...] + lnb_ref[...]
    y = y * jax.nn.sigmoid(y)
    y_ref[...] = _dot(y.astype(BF16), wpw_ref[...]).astype(BF16)
    st_ref[...] = ext[tt:tt + hp, :]


def _conv(pa, hist, wdw, bdw, lng, lnb, wpw, *, batch, seq, col0):
    m = pa.shape[0]
    width, d = wpw.shape
    assert col0 % width == 0
    tt = _pick(seq, 256)
    assert tt >= CONV_HIST_PAD and tt % CONV_ROW_BLOCK == 0
    nt = seq // tt
    cb = col0 // width
    return pl.pallas_call(
        functools.partial(_conv_kernel, tt=tt),
        grid=(batch, nt),
        in_specs=[pl.BlockSpec((tt, width), lambda b, t: (b * nt + t, cb)),
                  pl.BlockSpec((tt, width), lambda b, t: (b * nt + t, cb + 1)),
                  pl.BlockSpec((None, CONV_HIST_PAD, width), lambda b, t: (b, 0, 0)),
                  _const_spec(wdw.shape), _const_spec(bdw.shape), _const_spec(lng.shape),
                  _const_spec(lnb.shape), _const_spec(wpw.shape)],
        out_specs=[pl.BlockSpec((tt, d), lambda b, t: (b * nt + t, 0)),
                   pl.BlockSpec((None, CONV_HIST_PAD, width), lambda b, t: (b, 0, 0))],
        out_shape=[jax.ShapeDtypeStruct((m, d), BF16),
                   jax.ShapeDtypeStruct((batch, CONV_HIST_PAD, width), F32)],
        scratch_shapes=[pltpu.VMEM((CONV_HIST_PAD + tt, width), F32),
                        pltpu.VMEM((CONV_SUBLANES - 1, CONV_HIST_PAD + tt, width), F32),
                        pltpu.VMEM((tt, width), F32)],
        compiler_params=_params(2), name="conv",
    )(pa, pa, hist, wdw, bdw, lng, lnb, wpw)


def _merge_kernel(g0_ref, g1_ref, g2_ref, bg_ref, yp_ref, yc_ref, o_ref, wo_ref, wout_ref, x_ref,
                  gpost_ref, gnext_ref, xo_ref, h_ref):
    d = x_ref.shape[1]
    bg = bg_ref[...]
    gate0 = jax.nn.sigmoid(g0_ref[...].astype(F32) + bg[:, 0:d])
    gate1 = jax.nn.sigmoid(g1_ref[...].astype(F32) + bg[:, d:2 * d])
    gate2 = jax.nn.sigmoid(g2_ref[...].astype(F32) + bg[:, 2 * d:3 * d])
    y_mla = _dot(o_ref[...], wo_ref[...])
    merged = gate0 * yp_ref[...].astype(F32) + gate1 * yc_ref[...].astype(F32) + gate2 * y_mla
    z = _dot(merged.astype(BF16), wout_ref[...])
    xn = x_ref[...] + _rms(z, gpost_ref[...])
    xo_ref[...] = xn
    h_ref[...] = _rms(xn, gnext_ref[...]).astype(BF16)


def _merge(pa, bg, yp, yc, o, wo, wout, x, gpost, gnext):
    m, d = x.shape
    tm = _pick(m, 256)
    row = lambda i: (i, 0)
    return pl.pallas_call(
        _merge_kernel,
        grid=(m // tm,),
        in_specs=[pl.BlockSpec((tm, d), lambda i: (i, 0)), pl.BlockSpec((tm, d), lambda i: (i, 1)),
                  pl.BlockSpec((tm, d), lambda i: (i, 2)), _const_spec(bg.shape),
                  pl.BlockSpec((tm, d), row), pl.BlockSpec((tm, d), row),
                  pl.BlockSpec((tm, o.shape[1]), row), _const_spec(wo.shape),
                  _const_spec(wout.shape), pl.BlockSpec((tm, d), row),
                  _const_spec(gpost.shape), _const_spec(gnext.shape)],
        out_specs=[pl.BlockSpec((tm, d), row), pl.BlockSpec((tm, d), row)],
        out_shape=[jax.ShapeDtypeStruct((m, d), F32), jax.ShapeDtypeStruct((m, d), BF16)],
        compiler_params=_params(1), name="merge",
    )(pa, pa, pa, bg, yp, yc, o, wo, wout, x, gpost, gnext)


def _ffn_kernel(*refs, seg, nseg, nt, nk, emit_h):
    h_ref, hist_ref, wup_ref, wdw_ref, bdw_ref, wd_ref, x_ref, gpost_ref = refs[:8]
    rest = refs[8:]
    if emit_h:
        gnext_ref, xo_ref, hn_ref, tail_ref, carry, acc = rest
    else:
        xo_ref, tail_ref, carry, acc = rest
    i = pl.program_id(0)
    kk = pl.program_id(1)
    hp = FFN_HIST_PAD
    tk = wd_ref.shape[0]
    tm = h_ref.shape[0]
    rc = min(tm, FFN_ROW_CHUNK)
    sr = min(seg, rc)
    bias = bdw_ref[...]
    taps = [wdw_ref[k:k + 1, :] for k in range(FFN_K)]

    @pl.when(kk == 0)
    def _():
        acc[...] = jnp.zeros_like(acc)

    prev = None
    for c in range(tm // rc):
        rows = slice(c * rc, (c + 1) * rc)
        u = _dot(h_ref[rows, :], wup_ref[...])
        parts = []
        for p in range(rc // sr):
            r0 = c * rc + p * sr
            s = r0 // seg
            cur = u[p * sr:(p + 1) * sr, :]
            if r0 % seg == 0:
                prev = hist_ref[s]
                if nt > 1:
                    prev = jnp.where((i % nt) == 0, prev, carry[kk])
            ext = jnp.concatenate([prev, cur], axis=0)
            uc = bias + taps[FFN_K - 1] * cur
            for k in range(FFN_K - 1):
                o = hp - FFN_HIST + k
                uc = uc + taps[k] * ext[o:o + sr, :]
            parts.append((jax.nn.gelu(uc[:, :tk], approximate=True) * uc[:, tk:]).astype(BF16))
            prev = cur[sr - hp:, :]
            if (r0 + sr) % seg == 0:
                tail_ref[s] = prev
                if nt > 1:
                    carry[kk] = prev
        f = parts[0] if len(parts) == 1 else jnp.concatenate(parts, axis=0)
        acc[rows, :] += _dot(f, wd_ref[...])

    @pl.when(kk == nk - 1)
    def _():
        xn = x_ref[...] + _rms(acc[...], gpost_ref[...])
        xo_ref[...] = xn
        if emit_h:
            hn_ref[...] = _rms(xn, gnext_ref[...]).astype(BF16)


def _ffn_tile(f):
    return _pick(f, 512)


def _interleave_ffn_cols(a, f):
    tk = _ffn_tile(f)
    lead = a.shape[:-1]
    a = a.reshape(lead + (2, f // tk, tk))
    return jnp.swapaxes(a, -3, -2).reshape(lead + (2 * f,))


def _deinterleave_ffn_cols(a, f):
    tk = _ffn_tile(f)
    lead = a.shape[:-1]
    a = a.reshape(lead + (f // tk, 2, tk))
    return jnp.swapaxes(a, -3, -2).reshape(lead + (2 * f,))


def _ffn(h, hist, wup, wdw, bdw, wd, x, gpost, gnext, *, seq):
    m, d = x.shape
    f = wd.shape[0]
    emit_h = gnext is not None
    tm = _pick(m, 1024)
    if tm >= seq:
        assert tm % seq == 0
        seg, nseg, nt = seq, tm // seq, 1
    else:
        assert seq % tm == 0
        seg, nseg, nt = tm, 1, seq // tm
    assert seg >= FFN_HIST_PAD and (seg % FFN_ROW_CHUNK == 0 or FFN_ROW_CHUNK % seg == 0)
    tk = _ffn_tile(f)
    nk = f // tk
    row = lambda i, k: (i, 0)
    col = lambda i, k: (0, k)
    one = pl.Buffered(1)
    in_specs = [pl.BlockSpec((tm, d), row),
                pl.BlockSpec((nseg, FFN_HIST_PAD, 2 * tk), lambda i, k: (i // nt, 0, k)),
                pl.BlockSpec((d, 2 * tk), col), pl.BlockSpec((FFN_HIST_PAD, 2 * tk), col),
                pl.BlockSpec((1, 2 * tk), col), pl.BlockSpec((tk, d), lambda i, k: (k, 0)),
                pl.BlockSpec((tm, d), row, pipeline_mode=one), _const_spec(gpost.shape)]
    args = [h, hist, wup, wdw, bdw, wd, x, gpost]
    out_specs = [pl.BlockSpec((tm, d), row, pipeline_mode=one)]
    out_shape = [jax.ShapeDtypeStruct((m, d), F32)]
    if emit_h:
        in_specs.append(_const_spec(gnext.shape))
        args.append(gnext)
        out_specs.append(pl.BlockSpec((tm, d), row, pipeline_mode=one))
        out_shape.append(jax.ShapeDtypeStruct((m, d), BF16))
    out_specs.append(pl.BlockSpec((nseg, FFN_HIST_PAD, 2 * tk), lambda i, k: (i, 0, k)))
    out_shape.append(jax.ShapeDtypeStruct((m // seg, FFN_HIST_PAD, 2 * f), F32))
    return pl.pallas_call(
        functools.partial(_ffn_kernel, seg=seg, nseg=nseg, nt=nt, nk=nk, emit_h=emit_h),
        grid=(m // tm, nk), in_specs=in_specs, out_specs=out_specs, out_shape=out_shape,
        scratch_shapes=[pltpu.VMEM((nk, FFN_HIST_PAD, 2 * tk), F32), pltpu.VMEM((tm, d), F32)],
        compiler_params=_params(2), name="ffn",
    )(*args)


def _rotate_half_cols(w):
    half = w.shape[-1] // 2
    return jnp.concatenate([-w[..., half:], w[..., :half]], axis=-1)


def _pad_last(w, n):
    return jnp.pad(w, [(0, 0)] * (w.ndim - 1) + [(0, n - w.shape[-1])])


def _prep_layer(l, dims, w_in, w_uq, w_uk, w_uv, w_o_mla, w_pool, w_conv_dw, w_conv_pw, w_out,
                w_up, w_ffn_dw, w_down):
    d, pw, cw, ql, kl, rope, nope, nh = dims
    o_glu = pw
    o_q = o_glu + 2 * cw
    o_kv = o_q + ql
    o_kr = o_kv + kl
    o_gate = o_kr + rope
    wi = w_in[l]
    wa = jnp.concatenate([wi[:, o_gate:], wi[:, o_glu:o_q], wi[:, :o_glu]], axis=1).astype(BF16)
    wkr = wi[:, o_kr:o_gate]
    wb = jnp.concatenate([wi[:, o_q:o_kr], _pad_last(wkr, LANE),
                          _pad_last(_rotate_half_cols(wkr), LANE)], axis=1).astype(BF16)
    uq = w_uq[l]
    uq_r = uq[..., nope:]
    wq = jnp.concatenate([uq[..., :nope].reshape(ql, nh * nope),
                          _pad_last(uq_r, LANE).reshape(ql, nh * LANE),
                          _pad_last(_rotate_half_cols(uq_r), LANE).reshape(ql, nh * LANE)],
                         axis=1).astype(BF16)
    uk, uv = w_uk[l], w_uv[l]
    return dict(
        wa=wa, wb=wb, wq=wq,
        wuk=uk.reshape(kl, -1).astype(BF16), wuv=uv.reshape(kl, -1).astype(BF16),
        wuk_h=jnp.transpose(uk, (1, 2, 0)).astype(BF16), wuv_h=jnp.transpose(uv, (1, 0, 2)).astype(BF16),
        wo=w_o_mla[l].astype(BF16), wpool=w_pool[l].astype(BF16),
        wdw=_pad_rows(w_conv_dw[l], CONV_HIST_PAD), wpw=w_conv_pw[l].astype(BF16),
        wout=w_out[l].astype(BF16), wup=_interleave_ffn_cols(w_up[l], w_down.shape[1]).astype(BF16),
        wfdw=_interleave_ffn_cols(_pad_rows(w_ffn_dw[l], FFN_HIST_PAD), w_down.shape[1]),
        wdown=w_down[l].astype(BF16))


def _pad_rows(w, n):
    return jnp.pad(w, [(0, n - w.shape[0])] + [(0, 0)] * (w.ndim - 1))


def _pad_front(s, n):
    return jnp.pad(s, [(0, 0), (n - s.shape[1], 0), (0, 0)])


def _rope_tables(pos0, seq, rope):
    half = rope // 2
    inv = ROPE_THETA ** (-jnp.arange(half, dtype=F32) / half)
    pos = pos0 + jnp.arange(seq, dtype=jnp.int32)
    ang = pos.astype(F32)[:, None] * inv[None, :]
    cos, sin = jnp.cos(ang), jnp.sin(ang)
    return (_pad_last(jnp.concatenate([cos, cos], axis=1), LANE),
            _pad_last(jnp.concatenate([sin, sin], axis=1), LANE))


def _trunk(x3, pos0, cache_ckv, cache_kr, state_pool, state_conv, state_ffn, layers, vecs, dims):
    batch, seq, d = x3.shape
    _, pw, cw, ql, kl, rope, nope, nh = dims
    depth = len(layers)
    m = batch * seq
    x = x3.reshape(m, d)
    cached = cache_ckv is not None
    scale = 1.0 / math.sqrt(nope + rope)
    if not cached:
        scale *= math.log2(math.e)
    cos_t, sin_t = _rope_tables(pos0, seq, rope)
    col_glu = 3 * d
    col_pool = 3 * d + 2 * cw
    row = lambda v: v.reshape(1, -1)
    h = _norm_cast(x, row(vecs["g_pre_mix"][0]))
    outs = {k: [] for k in ("ckv", "kr", "pool", "conv", "ffn")}
    for l in range(depth):
        w = layers[l]
        f = w["wdown"].shape[0]
        pa = _matmul(h, w["wa"])
        res = _mla_prep(h, w["wb"], row(vecs["g_q_a"][l]), row(vecs["g_kv_a"][l]), cos_t, sin_t,
                        w["wq"], w["wuk"], w["wuv"], seq=seq, n_heads=nh, nope=nope, rope=rope,
                        scale=scale, expand_kv=not cached)
        if cached:
            ckv, kr, q = res
            o = _cached_attn(q, cache_ckv, cache_kr, ckv, kr, w["wuk_h"], w["wuv_h"], layer=l,
                             batch=batch, seq=seq, n_heads=nh, nope=nope, rope=rope)
            hist_pool = _pad_front(state_pool[l], POOL_HIST_PAD)
            hist_conv = _pad_front(state_conv[l], CONV_HIST_PAD)
            hist_ffn = _interleave_ffn_cols(_pad_front(state_ffn[l], FFN_HIST_PAD), f)
        else:
            ckv, kr, q, kfull, vfull = res
            o = _flash(q, kfull, vfull, batch=batch, seq=seq, n_heads=nh, nope=nope)
            hist_pool = jnp.zeros((batch, POOL_HIST_PAD, pw), F32)
            hist_conv = jnp.zeros((batch, CONV_HIST_PAD, cw), F32)
            hist_ffn = jnp.zeros((batch, FFN_HIST_PAD, w["wfdw"].shape[1]), F32)
        yp, st_pool = _pool(pa, hist_pool, w["wpool"], row(vecs["pool_scale"][l]), batch=batch,
                            seq=seq, col0=col_pool, pos0=pos0)
        yc, st_conv = _conv(pa, hist_conv, w["wdw"], row(vecs["b_conv_dw"][l]),
                            row(vecs["g_conv_ln"][l]), row(vecs["b_conv_ln"][l]), w["wpw"],
                            batch=batch, seq=seq, col0=col_glu)
        x, h2 = _merge(pa, row(vecs["b_gate"][l]), yp, yc, o, w["wo"], w["wout"], x,
                       row(vecs["g_post_mix"][l]), row(vecs["g_pre_ffn"][l]))
        g_next = row(vecs["g_pre_mix"][l + 1]) if l + 1 < depth else None
        res = _ffn(h2, hist_ffn, w["wup"], w["wfdw"],
                   row(_interleave_ffn_cols(vecs["b_ffn_dw"][l], f)), w["wdown"], x,
                   row(vecs["g_post_ffn"][l]), g_next, seq=seq)
        if g_next is None:
            x, tails = res
        else:
            x, h, tails = res
        outs["ckv"].append(ckv.reshape(batch, seq, kl))
        outs["kr"].append(kr.reshape(batch, seq, rope))
        outs["pool"].append(st_pool[:, POOL_HIST_PAD - POOL_HIST:])
        outs["conv"].append(st_conv[:, CONV_HIST_PAD - CONV_HIST:])
        tails = _deinterleave_ffn_cols(tails, f)
        tails = tails.reshape(batch, -1, FFN_HIST_PAD, 2 * f)[:, -1]
        outs["ffn"].append(tails[:, FFN_HIST_PAD - FFN_HIST:])
    return (x.reshape(batch, seq, d),) + tuple(jnp.stack(outs[k]) for k in ("ckv", "kr", "pool", "conv", "ffn"))


def kernel(x_prompt, x_sample, cache_ckv, cache_krope, state_pool, state_conv, state_ffn,
           g_pre_mix, w_in, b_gate, g_q_a, g_kv_a, w_uq, w_uk, w_uv, w_o_mla,
           w_pool, pool_scale, w_conv_dw, b_conv_dw, g_conv_ln, b_conv_ln, w_conv_pw,
           w_out, g_post_mix, g_pre_ffn, w_up, w_ffn_dw, b_ffn_dw, w_down, g_post_ffn):
    depth, d = g_pre_mix.shape
    nh = w_uq.shape[2]
    nope = w_uk.shape[3]
    rope = w_uq.shape[3] - nope
    dims = (d, state_pool.shape[3], state_conv.shape[3], w_uq.shape[1], w_uk.shape[1], rope, nope, nh)
    layers = [_prep_layer(l, dims, w_in, w_uq, w_uk, w_uv, w_o_mla, w_pool, w_conv_dw, w_conv_pw,
                          w_out, w_up, w_ffn_dw, w_down) for l in range(depth)]
    vecs = dict(g_pre_mix=g_pre_mix, b_gate=b_gate, g_q_a=g_q_a, g_kv_a=g_kv_a,
                pool_scale=pool_scale, b_conv_dw=b_conv_dw, g_conv_ln=g_conv_ln,
                b_conv_ln=b_conv_ln, g_post_mix=g_post_mix, g_pre_ffn=g_pre_ffn,
                b_ffn_dw=b_ffn_dw, g_post_ffn=g_post_ffn)
    prompt = _trunk(x_prompt, 0, None, None, None, None, None, layers, vecs, dims)
    sample = _trunk(x_sample, cache_ckv.shape[2], cache_ckv, cache_krope, state_pool, state_conv,
                    state_ffn, layers, vecs, dims)
    return (prompt[0], sample[0]) + prompt[1:] + sample[1:]
```

```python
import functools
import math

import jax
import jax.numpy as jnp
from jax import lax
from jax.experimental import pallas as pl
from jax.experimental.pallas import tpu as pltpu

CHUNK = 64
EPS = 1e-6
NEG_INF = -1e30
POOL_WINDOWS = (2, 4, 8, 16)
POOL_HIST = max(POOL_WINDOWS) - 1
POOL_HIST_PAD = 16
CONV_K = 31
CONV_HIST = CONV_K - 1
CONV_HIST_PAD = 32
CONV_SUBLANES = 8
CONV_ROW_BLOCK = 32
MIX_ROW_BLOCK = 16
FFN_K = 3
FFN_HIST = FFN_K - 1
FFN_HIST_PAD = 8
FFN_ROW_CHUNK = 512
ROPE_THETA = 10000.0
LANE = 128
V7X_VMEM_LIMIT = 60 * 1024 * 1024

F32 = jnp.float32
BF16 = jnp.bfloat16


def _params(n_axes):
    return pltpu.CompilerParams(dimension_semantics=("arbitrary",) * n_axes,
                                vmem_limit_bytes=V7X_VMEM_LIMIT)


def _const_spec(shape):
    zeros = (0,) * len(shape)
    return pl.BlockSpec(shape, lambda *_: zeros, pipeline_mode=pl.Buffered(1))


def _dot(a, b):
    return jnp.dot(a, b, preferred_element_type=F32)


def _dot_nt(a, b):
    return lax.dot_general(a, b, (((1,), (1,)), ((), ())), preferred_element_type=F32)


def _rms(x, g):
    return x * lax.rsqrt(jnp.mean(x * x, axis=-1, keepdims=True) + EPS) * g


def _pick(n, pref):
    t = min(n, pref)
    while n % t:
        t -= 1
    return t


def _norm_kernel(x_ref, g_ref, h_ref):
    h_ref[...] = _rms(x_ref[...], g_ref[...]).astype(BF16)


def _norm_cast(x, g):
    m, d = x.shape
    tm = _pick(m, 512)
    return pl.pallas_call(
        _norm_kernel,
        grid=(m // tm,),
        in_specs=[pl.BlockSpec((tm, d), lambda i: (i, 0)), _const_spec((1, d))],
        out_specs=pl.BlockSpec((tm, d), lambda i: (i, 0)),
        out_shape=jax.ShapeDtypeStruct((m, d), BF16),
        compiler_params=_params(1),
        name="norm_cast",
    )(x, g)


def _mm_kernel(x_ref, w_ref, o_ref):
    o_ref[...] = _dot(x_ref[...], w_ref[...]).astype(o_ref.dtype)


def _matmul(x, w, tm_pref=1024, tn_pref=2304):
    m, k = x.shape
    n = w.shape[1]
    tm = _pick(m, tm_pref)
    tn = _pick(n, tn_pref)
    return pl.pallas_call(
        _mm_kernel,
        grid=(m // tm, n // tn),
        in_specs=[pl.BlockSpec((tm, k), lambda i, j: (i, 0)),
                  pl.BlockSpec((k, tn), lambda i, j: (0, j))],
        out_specs=pl.BlockSpec((tm, tn), lambda i, j: (i, j)),
        out_shape=jax.ShapeDtypeStruct((m, n), BF16),
        compiler_params=_params(2),
        name="matmul",
    )(x, w)


def _mla_prep_kernel(h_ref, wb_ref, gq_ref, gkv_ref, cos_ref, sin_ref, wq_ref, *rest,
                     n_heads, q_lora, kv_lora, nope, rope, scale, expand_kv):
    if expand_kv:
        wuk_ref, wuv_ref, ckv_ref, kr_ref, q_ref, k_ref, v_ref = rest
    else:
        ckv_ref, kr_ref, q_ref = rest
    hn = n_heads * nope
    pb = _dot(h_ref[...], wb_ref[...])
    cq = _rms(pb[:, :q_lora], gq_ref[...])
    ckv = _rms(pb[:, q_lora:q_lora + kv_lora], gkv_ref[...])
    cos = cos_ref[...]
    sin = sin_ref[...]
    o = q_lora + kv_lora
    kr = pb[:, o:o + LANE] * cos + pb[:, o + LANE:o + 2 * LANE] * sin
    ckv_ref[...] = ckv
    kr_ref[...] = kr[:, :rope]
    qall = _dot(cq.astype(BF16), wq_ref[...])
    for hh in range(n_heads):
        c0 = hh * LANE
        qn = qall[:, c0:c0 + nope] * scale
        qr = (qall[:, hn + c0:hn + c0 + LANE] * cos
              + qall[:, 2 * hn + c0:2 * hn + c0 + LANE] * sin) * scale
        q_ref[:, 2 * c0:2 * c0 + nope] = qn.astype(BF16)
        q_ref[:, 2 * c0 + nope:2 * c0 + nope + LANE] = qr.astype(BF16)
    if expand_kv:
        ckvb = ckv.astype(BF16)
        kn = _dot(ckvb, wuk_ref[...])
        v_ref[...] = _dot(ckvb, wuv_ref[...]).astype(BF16)
        krb = kr.astype(BF16)
        for hh in range(n_heads):
            c0 = hh * LANE
            k_ref[:, 2 * c0:2 * c0 + nope] = kn[:, c0:c0 + nope].astype(BF16)
            k_ref[:, 2 * c0 + nope:2 * c0 + nope + LANE] = krb


def _mla_prep(h, wb, gq, gkv, cos_t, sin_t, wq, wuk, wuv, *, seq, n_heads, nope, rope, scale,
              expand_kv):
    m, d = h.shape
    q_lora, kv_lora = gq.shape[1], gkv.shape[1]
    assert nope == LANE and rope <= LANE
    tm = _pick(seq, 512)
    nt = seq // tm
    hq = n_heads * (nope + LANE)
    row = lambda i: (i, 0)
    tab = lambda i: (i % nt, 0)
    in_specs = [pl.BlockSpec((tm, d), row), _const_spec(wb.shape), _const_spec(gq.shape),
                _const_spec(gkv.shape), pl.BlockSpec((tm, LANE), tab), pl.BlockSpec((tm, LANE), tab),
                _const_spec(wq.shape)]
    args = [h, wb, gq, gkv, cos_t, sin_t, wq]
    out_specs = [pl.BlockSpec((tm, kv_lora), row), pl.BlockSpec((tm, rope), row),
                 pl.BlockSpec((tm, hq), row)]
    out_shape = [jax.ShapeDtypeStruct((m, kv_lora), F32), jax.ShapeDtypeStruct((m, rope), F32),
                 jax.ShapeDtypeStruct((m, hq), BF16)]
    if expand_kv:
        in_specs += [_const_spec(wuk.shape), _const_spec(wuv.shape)]
        args += [wuk, wuv]
        out_specs += [pl.BlockSpec((tm, hq), row), pl.BlockSpec((tm, n_heads * nope), row)]
        out_shape += [jax.ShapeDtypeStruct((m, hq), BF16),
                      jax.ShapeDtypeStruct((m, n_heads * nope), BF16)]
    return pl.pallas_call(
        functools.partial(_mla_prep_kernel, n_heads=n_heads, q_lora=q_lora, kv_lora=kv_lora,
                          nope=nope, rope=rope, scale=scale, expand_kv=expand_kv),
        grid=(m // tm,), in_specs=in_specs, out_specs=out_specs, out_shape=out_shape,
        compiler_params=_params(1), name="mla_prep",
    )(*args)


def _flash_kernel(q_ref, k_ref, v_ref, o_ref, *, tq, heads, dv):
    dk = 2 * LANE
    nq = q_ref.shape[0] // tq
    r = lax.broadcasted_iota(jnp.int32, (tq, tq), 0) // CHUNK
    c = lax.broadcasted_iota(jnp.int32, (tq, tq), 1) // CHUNK
    visible = c <= r
    for qi in range(nq):
        rows = slice(qi * tq, (qi + 1) * tq)
        carry = [(jnp.full((tq, 1), NEG_INF, F32), jnp.zeros((tq, 1), F32),
                  jnp.zeros((tq, dv), F32)) for _ in range(heads)]
        for j in range(qi + 1):
            keys = slice(j * tq, (j + 1) * tq)
            for hh in range(heads):
                m, l, acc = carry[hh]
                s = _dot_nt(q_ref[rows, hh * dk:(hh + 1) * dk],
                            k_ref[keys, hh * dk:(hh + 1) * dk])
                if j == qi:
                    s = jnp.where(visible, s, NEG_INF)
                m_new = jnp.maximum(m, jnp.max(s, axis=-1, keepdims=True))
                p = jnp.exp2(s - m_new)
                alpha = jnp.exp2(m - m_new)
                l = alpha * l + jnp.sum(p, axis=-1, keepdims=True)
                acc = alpha * acc + _dot(p.astype(BF16), v_ref[keys, hh * dv:(hh + 1) * dv])
                carry[hh] = (m_new, l, acc)
        for hh, (_, l, acc) in enumerate(carry):
            o_ref[rows, hh * dv:(hh + 1) * dv] = (acc / l).astype(BF16)


def _flash(q, k, v, *, batch, seq, n_heads, nope):
    m = q.shape[0]
    tq = _pick(seq, 512)
    assert tq % CHUNK == 0
    heads = _pick(n_heads, 2)
    dk = 2 * LANE
    blk = lambda b, h: (b, h)
    return pl.pallas_call(
        functools.partial(_flash_kernel, tq=tq, heads=heads, dv=nope),
        grid=(batch, n_heads // heads),
        in_specs=[pl.BlockSpec((seq, heads * dk), blk), pl.BlockSpec((seq, heads * dk), blk),
                  pl.BlockSpec((seq, heads * nope), blk)],
        out_specs=pl.BlockSpec((seq, heads * nope), blk),
        out_shape=jax.ShapeDtypeStruct((m, n_heads * nope), BF16),
        compiler_params=_params(2), name="flash",
    )(q, k, v)


def _cached_attn_kernel(q_ref, cc_ref, kc_ref, cn_ref, kn_ref, wuk_ref, wuv_ref, o_ref,
                        qlat_scr, qr_scr, *, n_heads, nope, rope, seq):
    ckv_c = cc_ref[...].astype(BF16)
    kr_c = kc_ref[...].astype(BF16)
    ckv_n = cn_ref[...].astype(BF16)
    kr_n = kn_ref[...].astype(BF16)
    for hh in range(n_heads):
        c0 = hh * 2 * LANE
        qlat_scr[hh * seq:(hh + 1) * seq, :] = _dot(q_ref[:, c0:c0 + nope], wuk_ref[hh]).astype(BF16)
        qr_scr[hh * seq:(hh + 1) * seq, :] = q_ref[:, c0 + nope:c0 + nope + rope]
    qlat = qlat_scr[...]
    qr = qr_scr[...]
    s_c = _dot_nt(qlat, ckv_c) + _dot_nt(qr, kr_c)
    s_n = _dot_nt(qlat, ckv_n) + _dot_nt(qr, kr_n)
    mx = jnp.maximum(jnp.max(s_c, axis=-1, keepdims=True), jnp.max(s_n, axis=-1, keepdims=True))
    p_c = jnp.exp(s_c - mx)
    p_n = jnp.exp(s_n - mx)
    den = jnp.sum(p_c, axis=-1, keepdims=True) + jnp.sum(p_n, axis=-1, keepdims=True)
    o_lat = ((_dot(p_c.astype(BF16), ckv_c) + _dot(p_n.astype(BF16), ckv_n)) / den).astype(BF16)
    for hh in range(n_heads):
        o_ref[:, hh * nope:(hh + 1) * nope] = _dot(
            o_lat[hh * seq:(hh + 1) * seq, :], wuv_ref[hh]).astype(BF16)


def _cached_attn(q, cache_ckv, cache_kr, ckv_new, kr_new, wuk_h, wuv_h, *, layer, batch, seq,
                 n_heads, nope, rope):
    past, kv_lora = cache_ckv.shape[2], cache_ckv.shape[3]
    assert seq <= CHUNK and past % CHUNK == 0
    hq = q.shape[1]
    return pl.pallas_call(
        functools.partial(_cached_attn_kernel, n_heads=n_heads, nope=nope, rope=rope, seq=seq),
        grid=(batch,),
        in_specs=[pl.BlockSpec((seq, hq), lambda b: (b, 0)),
                  pl.BlockSpec((None, None, past, kv_lora), lambda b: (layer, b, 0, 0)),
                  pl.BlockSpec((None, None, past, rope), lambda b: (layer, b, 0, 0)),
                  pl.BlockSpec((seq, kv_lora), lambda b: (b, 0)),
                  pl.BlockSpec((seq, rope), lambda b: (b, 0)),
                  _const_spec(wuk_h.shape), _const_spec(wuv_h.shape)],
        out_specs=pl.BlockSpec((seq, n_heads * nope), lambda b: (b, 0)),
        out_shape=jax.ShapeDtypeStruct((batch * seq, n_heads * nope), BF16),
        scratch_shapes=[pltpu.VMEM((n_heads * seq, kv_lora), BF16),
                        pltpu.VMEM((n_heads * seq, rope), BF16)],
        compiler_params=_params(1), name="cached_attn",
    )(q, cache_ckv, cache_kr, ckv_new, kr_new, wuk_h, wuv_h)


def _mixer_kernel(g0_ref, g1_ref, g2_ref, a_ref, gate_ref, u_ref, o_ref, x_ref, hp_ref, hc_ref,
                  bg_ref, wpool_ref, pscale_ref, wdw_ref, bdw_ref, lng_ref, lnb_ref, wpw_ref,
                  wo_ref, wout_ref, gpost_ref, gnext_ref,
                  xo_ref, h_ref, stp_ref, stc_ref, pext, cext, shifted, dw_out, merged_scr,
                  z_scr, ymla_scr, yconv_scr, ypool_scr, act_scr, *, tt, nt, n_tiles, pos0):
    g = pl.program_id(0)
    ti = jnp.minimum(g, n_tiles - 1) % nt
    php, chp, sub = POOL_HIST_PAD, CONV_HIST_PAD, CONV_SUBLANES
    d = x_ref.shape[1]

    @pl.when(g == 0)
    def _():
        merged_scr[...] = jnp.zeros_like(merged_scr)

    @pl.when(ti == 0)
    def _():
        pext[0:php, :] = hp_ref[...]
        cext[0:chp, :] = hc_ref[...]

    @pl.when(ti > 0)
    def _():
        pext[0:php, :] = pext[tt:tt + php, :]
        cext[0:chp, :] = cext[tt:tt + chp, :]

    z_scr[...] = _dot(merged_scr[(g + 1) % 2], wout_ref[...])
    blocks = [slice(r0, r0 + MIX_ROW_BLOCK) for r0 in range(0, tt, MIX_ROW_BLOCK)]
    for rows in blocks:
        xn = x_ref[rows, :] + _rms(z_scr[rows, :], gpost_ref[...])
        xo_ref[rows, :] = xn
        h_ref[rows, :] = _rms(xn, gnext_ref[...]).astype(BF16)

    ymla_scr[...] = _dot(o_ref[...], wo_ref[...])

    cext[chp:chp + tt, :] = a_ref[...].astype(F32) * jax.nn.sigmoid(gate_ref[...].astype(F32))
    off = chp - CONV_HIST
    whole = cext[...]
    for r in range(1, sub):
        shifted[r - 1] = pltpu.roll(whole, chp + tt - r, axis=0)
    rb = min(tt, CONV_ROW_BLOCK)
    bias = jnp.broadcast_to(bdw_ref[...], (rb, cext.shape[1]))
    for b0 in range(0, tt, rb):
        acc = bias
        for k in range(CONV_K):
            a, r = divmod(off + k, sub)
            lo = b0 + sub * a
            src = cext[lo:lo + rb, :] if r == 0 else shifted[r - 1, lo:lo + rb, :]
            acc = acc + jnp.tile(wdw_ref[k], (rb // sub, 1)) * src
        dw_out[b0:b0 + rb, :] = acc
    for rows in blocks:
        acc = dw_out[rows, :]
        mu = jnp.mean(acc, axis=-1, keepdims=True)
        cen = acc - mu
        var = jnp.mean(cen * cen, axis=-1, keepdims=True)
        yc = cen * lax.rsqrt(var + EPS) * lng_ref[...] + lnb_ref[...]
        act_scr[rows, :] = (yc * jax.nn.sigmoid(yc)).astype(BF16)
    yconv_scr[...] = _dot(act_scr[...], wpw_ref[...])
    stc_ref[...] = cext[tt:tt + chp, :]

    pext[php:php + tt, :] = u_ref[...].astype(F32)
    pos = pos0 + ti * tt + lax.broadcasted_iota(jnp.int32, (tt, 1), 0)
    pg, og = wpool_ref.shape[1], wpool_ref.shape[2]
    for gi, w in enumerate(POOL_WINDOWS):
        tot = pext[:, gi * pg:(gi + 1) * pg]
        cur = tot[php:php + tt, :]
        span = 1
        while span < w:
            tot = tot + pltpu.roll(tot, span, axis=0)
            span *= 2
        cnt = jnp.minimum(pos + 1, w).astype(F32)
        dlt = tot[php:php + tt, :] / cnt - cur
        cols = slice(gi * og, (gi + 1) * og)
        ypool_scr[:, cols] = _dot(dlt.astype(BF16), wpool_ref[gi]) * pscale_ref[:, cols]
    stp_ref[...] = pext[tt:tt + php, :]

    hb = 0.5 * bg_ref[...]
    for rows in blocks:
        total = None
        for br, (g_ref, y_scr) in enumerate(((g0_ref, ypool_scr), (g1_ref, yconv_scr),
                                             (g2_ref, ymla_scr))):
            th = jnp.tanh(0.5 * g_ref[rows, :].astype(F32) + hb[:, br * d:(br + 1) * d])
            y = y_scr[rows, :]
            part = y + th * y
            total = part if total is None else total + part
        merged_scr[g % 2, rows, :] = (0.5 * total).astype(BF16)


def _mixer(pa, o, x, hist_pool, hist_conv, bg, w_pool, pscale, wdw, bdw, lng, lnb, wpw, wo, wout,
           gpost, gnext, *, batch, seq, pos0):
    m, d = x.shape
    g, pg, og = w_pool.shape
    pw = g * pg
    cw = wpw.shape[0]
    col_glu, col_pool = 3 * d, 3 * d + 2 * cw
    assert col_glu % cw == 0 and col_pool % pw == 0
    assert all(w & (w - 1) == 0 and w <= POOL_HIST_PAD for w in POOL_WINDOWS)
    tt = _pick(seq, 256)
    rb = min(tt, CONV_ROW_BLOCK)
    assert tt >= CONV_HIST_PAD and tt >= POOL_HIST_PAD and tt % rb == 0
    assert tt % MIX_ROW_BLOCK == 0
    nt = seq // tt
    n_tiles = batch * nt
    one = pl.Buffered(1)
    cur = lambda c: (lambda i: (jnp.minimum(i, n_tiles - 1), c))
    prev = lambda i: (jnp.maximum(i - 1, 0), 0)
    stream = lambda i: (jnp.minimum(i, n_tiles - 1) // nt, 0, 0)
    consts = [bg, w_pool, pscale, wdw, bdw, lng, lnb, wpw, wo, wout, gpost, gnext]
    return pl.pallas_call(
        functools.partial(_mixer_kernel, tt=tt, nt=nt, n_tiles=n_tiles, pos0=pos0),
        grid=(n_tiles + 1,),
        in_specs=[pl.BlockSpec((tt, d), cur(0)), pl.BlockSpec((tt, d), cur(1)),
                  pl.BlockSpec((tt, d), cur(2)),
                  pl.BlockSpec((tt, cw), cur(col_glu // cw)),
                  pl.BlockSpec((tt, cw), cur(col_glu // cw + 1)),
                  pl.BlockSpec((tt, pw), cur(col_pool // pw)),
                  pl.BlockSpec((tt, o.shape[1]), cur(0)), pl.BlockSpec((tt, d), prev),
                  pl.BlockSpec((None, POOL_HIST_PAD, pw), stream),
                  pl.BlockSpec((None, CONV_HIST_PAD, cw), stream)]
                 + [_const_spec(c.shape) for c in consts],
        out_specs=[pl.BlockSpec((tt, d), prev, pipeline_mode=one),
                   pl.BlockSpec((tt, d), prev, pipeline_mode=one),
                   pl.BlockSpec((None, POOL_HIST_PAD, pw), stream),
                   pl.BlockSpec((None, CONV_HIST_PAD, cw), stream)],
        out_shape=[jax.ShapeDtypeStruct((m, d), F32), jax.ShapeDtypeStruct((m, d), BF16),
                   jax.ShapeDtypeStruct((batch, POOL_HIST_PAD, pw), F32),
                   jax.ShapeDtypeStruct((batch, CONV_HIST_PAD, cw), F32)],
        scratch_shapes=[pltpu.VMEM((POOL_HIST_PAD + tt, pw), F32),
                        pltpu.VMEM((CONV_HIST_PAD + tt, cw), F32),
                        pltpu.VMEM((CONV_SUBLANES - 1, CONV_HIST_PAD + tt, cw), F32),
                        pltpu.VMEM((tt, cw), F32),
                        pltpu.VMEM((2, tt, d), BF16)]
                       + [pltpu.VMEM((tt, d), F32)] * 4 + [pltpu.VMEM((tt, cw), BF16)],
        compiler_params=_params(1), name="mixer",
    )(pa, pa, pa, pa, pa, pa, o, x, hist_pool, hist_conv, *consts)


def _ffn_kernel(*refs, seg, nseg, nt, nk, emit_h):
    h_ref, hist_ref, wup_ref, wdw_ref, bdw_ref, wd_ref, x_ref, gpost_ref = refs[:8]
    rest = refs[8:]
    if emit_h:
        gnext_ref, xo_ref, hn_ref, tail_ref, carry, acc = rest
    else:
        xo_ref, tail_ref, carry, acc = rest
    i = pl.program_id(0)
    kk = pl.program_id(1)
    hp = FFN_HIST_PAD
    tk = wd_ref.shape[0]
    tm = h_ref.shape[0]
    rc = min(tm, FFN_ROW_CHUNK)
    sr = min(seg, rc)
    bias = bdw_ref[...]
    taps = [wdw_ref[k:k + 1, :] for k in range(FFN_K)]

    @pl.when(kk == 0)
    def _():
        acc[...] = jnp.zeros_like(acc)

    ups = [_dot(h_ref[c * rc:(c + 1) * rc, :], wup_ref[...]) for c in range(tm // rc)]
    prev = None
    for c in range(tm // rc):
        rows = slice(c * rc, (c + 1) * rc)
        u = ups[c]
        parts = []
        for p in range(rc // sr):
            r0 = c * rc + p * sr
            s = r0 // seg
            cur = u[p * sr:(p + 1) * sr, :]
            if r0 % seg == 0:
                prev = hist_ref[s]
                if nt > 1:
                    prev = jnp.where((i % nt) == 0, prev, carry[kk])
            ext = jnp.concatenate([prev, cur], axis=0)
            uc = bias + taps[FFN_K - 1] * cur
            for k in range(FFN_K - 1):
                o = hp - FFN_HIST + k
                uc = uc + taps[k] * ext[o:o + sr, :]
            parts.append((jax.nn.gelu(uc[:, :tk], approximate=True) * uc[:, tk:]).astype(BF16))
            prev = cur[sr - hp:, :]
            if (r0 + sr) % seg == 0:
                tail_ref[s] = prev
                if nt > 1:
                    carry[kk] = prev
        f = parts[0] if len(parts) == 1 else jnp.concatenate(parts, axis=0)
        acc[rows, :] += _dot(f, wd_ref[...])

    @pl.when(kk == nk - 1)
    def _():
        xn = x_ref[...] + _rms(acc[...], gpost_ref[...])
        xo_ref[...] = xn
        if emit_h:
            hn_ref[...] = _rms(xn, gnext_ref[...]).astype(BF16)


def _ffn_tile(f):
    return _pick(f, 512)


def _interleave_ffn_cols(a, f):
    tk = _ffn_tile(f)
    lead = a.shape[:-1]
    a = a.reshape(lead + (2, f // tk, tk))
    return jnp.swapaxes(a, -3, -2).reshape(lead + (2 * f,))


def _cast_kernel(src_ref, dst_ref):
    dst_ref[...] = src_ref[...].astype(dst_ref.dtype)


def _interleave_cast_up(w_up, layer, f):
    d = w_up.shape[1]
    tk = _ffn_tile(f)
    nk = f // tk
    return pl.pallas_call(
        _cast_kernel,
        grid=(2 * nk,),
        in_specs=[pl.BlockSpec((None, d, tk), lambda j: (layer, 0, (j % 2) * nk + j // 2))],
        out_specs=pl.BlockSpec((d, tk), lambda j: (0, j)),
        out_shape=jax.ShapeDtypeStruct((d, 2 * f), BF16),
        compiler_params=_params(1), name="cast_up",
    )(w_up)


def _deinterleave_ffn_cols(a, f):
    tk = _ffn_tile(f)
    lead = a.shape[:-1]
    a = a.reshape(lead + (f // tk, 2, tk))
    return jnp.swapaxes(a, -3, -2).reshape(lead + (2 * f,))


def _ffn(h, hist, wup, wdw, bdw, wd, x, gpost, gnext, *, seq):
    m, d = x.shape
    f = wd.shape[0]
    emit_h = gnext is not None
    tm = _pick(m, 1024)
    if tm >= seq:
        assert tm % seq == 0
        seg, nseg, nt = seq, tm // seq, 1
    else:
        assert seq % tm == 0
        seg, nseg, nt = tm, 1, seq // tm
    assert seg >= FFN_HIST_PAD and (seg % FFN_ROW_CHUNK == 0 or FFN_ROW_CHUNK % seg == 0)
    tk = _ffn_tile(f)
    nk = f // tk
    row = lambda i, k: (i, 0)
    col = lambda i, k: (0, k)
    one = pl.Buffered(1)
    in_specs = [pl.BlockSpec((tm, d), row, pipeline_mode=one),
                pl.BlockSpec((nseg, FFN_HIST_PAD, 2 * tk), lambda i, k: (i // nt, 0, k)),
                pl.BlockSpec((d, 2 * tk), col), pl.BlockSpec((FFN_HIST_PAD, 2 * tk), col),
                pl.BlockSpec((1, 2 * tk), col), pl.BlockSpec((tk, d), lambda i, k: (k, 0)),
                pl.BlockSpec((tm, d), row, pipeline_mode=one), _const_spec(gpost.shape)]
    args = [h, hist, wup, wdw, bdw, wd, x, gpost]
    out_specs = [pl.BlockSpec((tm, d), row, pipeline_mode=one)]
    out_shape = [jax.ShapeDtypeStruct((m, d), F32)]
    if emit_h:
        in_specs.append(_const_spec(gnext.shape))
        args.append(gnext)
        out_specs.append(pl.BlockSpec((tm, d), row, pipeline_mode=one))
        out_shape.append(jax.ShapeDtypeStruct((m, d), BF16))
    out_specs.append(pl.BlockSpec((nseg, FFN_HIST_PAD, 2 * tk), lambda i, k: (i, 0, k)))
    out_shape.append(jax.ShapeDtypeStruct((m // seg, FFN_HIST_PAD, 2 * f), F32))
    return pl.pallas_call(
        functools.partial(_ffn_kernel, seg=seg, nseg=nseg, nt=nt, nk=nk, emit_h=emit_h),
        grid=(m // tm, nk), in_specs=in_specs, out_specs=out_specs, out_shape=out_shape,
        scratch_shapes=[pltpu.VMEM((nk, FFN_HIST_PAD, 2 * tk), F32), pltpu.VMEM((tm, d), F32)],
        compiler_params=_params(2), name="ffn",
    )(*args)


def _rotate_half_cols(w):
    half = w.shape[-1] // 2
    return jnp.concatenate([-w[..., half:], w[..., :half]], axis=-1)


def _pad_last(w, n):
    return jnp.pad(w, [(0, 0)] * (w.ndim - 1) + [(0, n - w.shape[-1])])


def _prep_layer(l, dims, w_in, w_uq, w_uk, w_uv, w_o_mla, w_pool, w_conv_dw, w_conv_pw, w_out,
                w_up, w_ffn_dw, w_down):
    d, pw, cw, ql, kl, rope, nope, nh = dims
    o_glu = pw
    o_q = o_glu + 2 * cw
    o_kv = o_q + ql
    o_kr = o_kv + kl
    o_gate = o_kr + rope
    wi = w_in[l]
    wa = jnp.concatenate([wi[:, o_gate:], wi[:, o_glu:o_q], wi[:, :o_glu]], axis=1).astype(BF16)
    wkr = wi[:, o_kr:o_gate]
    wb = jnp.concatenate([wi[:, o_q:o_kr], _pad_last(wkr, LANE),
                          _pad_last(_rotate_half_cols(wkr), LANE)], axis=1).astype(BF16)
    uq = w_uq[l]
    uq_r = uq[..., nope:]
    wq = jnp.concatenate([uq[..., :nope].reshape(ql, nh * nope),
                          _pad_last(uq_r, LANE).reshape(ql, nh * LANE),
                          _pad_last(_rotate_half_cols(uq_r), LANE).reshape(ql, nh * LANE)],
                         axis=1).astype(BF16)
    uk, uv = w_uk[l], w_uv[l]
    return dict(
        wa=wa, wb=wb, wq=wq,
        wuk=uk.reshape(kl, -1).astype(BF16), wuv=uv.reshape(kl, -1).astype(BF16),
        wuk_h=jnp.transpose(uk, (1, 2, 0)).astype(BF16), wuv_h=jnp.transpose(uv, (1, 0, 2)).astype(BF16),
        wo=w_o_mla[l].astype(BF16), wpool=w_pool[l].astype(BF16),
        wdw=_replicate_sublanes(w_conv_dw[l]), wpw=w_conv_pw[l].astype(BF16),
        wout=w_out[l].astype(BF16), wup=_interleave_cast_up(w_up, l, w_down.shape[1]),
        wfdw=_interleave_ffn_cols(_pad_rows(w_ffn_dw[l], FFN_HIST_PAD), w_down.shape[1]),
        wdown=w_down[l].astype(BF16))


def _replicate_sublanes(w):
    return jnp.broadcast_to(w[:, None, :], (w.shape[0], CONV_SUBLANES, w.shape[1]))


def _pad_rows(w, n):
    return jnp.pad(w, [(0, n - w.shape[0])] + [(0, 0)] * (w.ndim - 1))


def _pad_front(s, n):
    return jnp.pad(s, [(0, 0), (n - s.shape[1], 0), (0, 0)])


def _rope_tables(pos0, seq, rope):
    half = rope // 2
    inv = ROPE_THETA ** (-jnp.arange(half, dtype=F32) / half)
    pos = pos0 + jnp.arange(seq, dtype=jnp.int32)
    ang = pos.astype(F32)[:, None] * inv[None, :]
    cos, sin = jnp.cos(ang), jnp.sin(ang)
    return (_pad_last(jnp.concatenate([cos, cos], axis=1), LANE),
            _pad_last(jnp.concatenate([sin, sin], axis=1), LANE))


def _trunk(x3, pos0, cache_ckv, cache_kr, state_pool, state_conv, state_ffn, layers, vecs, dims):
    batch, seq, d = x3.shape
    _, pw, cw, ql, kl, rope, nope, nh = dims
    depth = len(layers)
    m = batch * seq
    x = x3.reshape(m, d)
    cached = cache_ckv is not None
    scale = 1.0 / math.sqrt(nope + rope)
    if not cached:
        scale *= math.log2(math.e)
    cos_t, sin_t = _rope_tables(pos0, seq, rope)
    row = lambda v: v.reshape(1, -1)
    h = _norm_cast(x, row(vecs["g_pre_mix"][0]))
    outs = {k: [] for k in ("ckv", "kr", "pool", "conv", "ffn")}
    for l in range(depth):
        w = layers[l]
        f = w["wdown"].shape[0]
        pa = _matmul(h, w["wa"])
        res = _mla_prep(h, w["wb"], row(vecs["g_q_a"][l]), row(vecs["g_kv_a"][l]), cos_t, sin_t,
                        w["wq"], w["wuk"], w["wuv"], seq=seq, n_heads=nh, nope=nope, rope=rope,
                        scale=scale, expand_kv=not cached)
        if cached:
            ckv, kr, q = res
            o = _cached_attn(q, cache_ckv, cache_kr, ckv, kr, w["wuk_h"], w["wuv_h"], layer=l,
                             batch=batch, seq=seq, n_heads=nh, nope=nope, rope=rope)
            hist_pool = _pad_front(state_pool[l], POOL_HIST_PAD)
            hist_conv = _pad_front(state_conv[l], CONV_HIST_PAD)
            hist_ffn = _interleave_ffn_cols(_pad_front(state_ffn[l], FFN_HIST_PAD), f)
        else:
            ckv, kr, q, kfull, vfull = res
            o = _flash(q, kfull, vfull, batch=batch, seq=seq, n_heads=nh, nope=nope)
            hist_pool = jnp.zeros((batch, POOL_HIST_PAD, pw), F32)
            hist_conv = jnp.zeros((batch, CONV_HIST_PAD, cw), F32)
            hist_ffn = jnp.zeros((batch, FFN_HIST_PAD, w["wfdw"].shape[1]), F32)
        x, h2, st_pool, st_conv = _mixer(
            pa, o, x, hist_pool, hist_conv, row(vecs["b_gate"][l]), w["wpool"],
            row(vecs["pool_scale"][l]), w["wdw"], row(vecs["b_conv_dw"][l]),
            row(vecs["g_conv_ln"][l]), row(vecs["b_conv_ln"][l]), w["wpw"], w["wo"], w["wout"],
            row(vecs["g_post_mix"][l]), row(vecs["g_pre_ffn"][l]), batch=batch, seq=seq, pos0=pos0)
        g_next = row(vecs["g_pre_mix"][l + 1]) if l + 1 < depth else None
        res = _ffn(h2, hist_ffn, w["wup"], w["wfdw"],
                   row(_interleave_ffn_cols(vecs["b_ffn_dw"][l], f)), w["wdown"], x,
                   row(vecs["g_post_ffn"][l]), g_next, seq=seq)
        if g_next is None:
            x, tails = res
        else:
            x, h, tails = res
        outs["ckv"].append(ckv.reshape(batch, seq, kl))
        outs["kr"].append(kr.reshape(batch, seq, rope))
        outs["pool"].append(st_pool[:, POOL_HIST_PAD - POOL_HIST:])
        outs["conv"].append(st_conv[:, CONV_HIST_PAD - CONV_HIST:])
        tails = _deinterleave_ffn_cols(tails, f)
        tails = tails.reshape(batch, -1, FFN_HIST_PAD, 2 * f)[:, -1]
        outs["ffn"].append(tails[:, FFN_HIST_PAD - FFN_HIST:])
    return (x.reshape(batch, seq, d),) + tuple(jnp.stack(outs[k]) for k in ("ckv", "kr", "pool", "conv", "ffn"))


def kernel(x_prompt, x_sample, cache_ckv, cache_krope, state_pool, state_conv, state_ffn,
           g_pre_mix, w_in, b_gate, g_q_a, g_kv_a, w_uq, w_uk, w_uv, w_o_mla,
           w_pool, pool_scale, w_conv_dw, b_conv_dw, g_conv_ln, b_conv_ln, w_conv_pw,
           w_out, g_post_mix, g_pre_ffn, w_up, w_ffn_dw, b_ffn_dw, w_down, g_post_ffn):
    depth, d = g_pre_mix.shape
    nh = w_uq.shape[2]
    nope = w_uk.shape[3]
    rope = w_uq.shape[3] - nope
    dims = (d, state_pool.shape[3], state_conv.shape[3], w_uq.shape[1], w_uk.shape[1], rope, nope, nh)
    layers = [_prep_layer(l, dims, w_in, w_uq, w_uk, w_uv, w_o_mla, w_pool, w_conv_dw, w_conv_pw,
                          w_out, w_up, w_ffn_dw, w_down) for l in range(depth)]
    vecs = dict(g_pre_mix=g_pre_mix, b_gate=b_gate, g_q_a=g_q_a, g_kv_a=g_kv_a,
                pool_scale=pool_scale, b_conv_dw=b_conv_dw, g_conv_ln=g_conv_ln,
                b_conv_ln=b_conv_ln, g_post_mix=g_post_mix, g_pre_ffn=g_pre_ffn,
                b_ffn_dw=b_ffn_dw, g_post_ffn=g_post_ffn)
    prompt = _trunk(x_prompt, 0, None, None, None, None, None, layers, vecs, dims)
    sample = _trunk(x_sample, cache_ckv.shape[2], cache_ckv, cache_krope, state_pool, state_conv,
                    state_ffn, layers, vecs, dims)
    return (prompt[0], sample[0]) + prompt[1:] + sample[1:]
```

```python
import functools
import math

import jax
import jax.numpy as jnp
from jax import lax
from jax.experimental import pallas as pl
from jax.experimental.pallas import tpu as pltpu

CHUNK = 64
EPS = 1e-6
NEG_INF = -1e30
POOL_WINDOWS = (2, 4, 8, 16)
POOL_HIST = max(POOL_WINDOWS) - 1
POOL_HIST_PAD = 16
CONV_K = 31
CONV_HIST = CONV_K - 1
CONV_HIST_PAD = 32
CONV_SUBLANES = 8
CONV_ROW_BLOCK = 32
MIX_ROW_BLOCK = 16
MIX_TILE_ROWS = 256
MLA_TILE_ROWS = 512
FFN_K = 3
FFN_HIST = FFN_K - 1
FFN_HIST_PAD = 8
FFN_ROW_CHUNK = 512
ROPE_THETA = 10000.0
LANE = 128
V7X_VMEM_LIMIT = 60 * 1024 * 1024

F32 = jnp.float32
BF16 = jnp.bfloat16


def _params(n_axes):
    return pltpu.CompilerParams(dimension_semantics=("arbitrary",) * n_axes,
                                vmem_limit_bytes=V7X_VMEM_LIMIT)


def _const_spec(shape):
    zeros = (0,) * len(shape)
    return pl.BlockSpec(shape, lambda *_: zeros, pipeline_mode=pl.Buffered(1))


def _dot(a, b):
    return jnp.dot(a, b, preferred_element_type=F32)


def _dot_nt(a, b):
    return lax.dot_general(a, b, (((1,), (1,)), ((), ())), preferred_element_type=F32)


def _rms(x, g):
    return x * lax.rsqrt(jnp.mean(x * x, axis=-1, keepdims=True) + EPS) * g


def _pick(n, pref):
    t = min(n, pref)
    while n % t:
        t -= 1
    return t


def _norm_kernel(x_ref, g_ref, h_ref):
    h_ref[...] = _rms(x_ref[...], g_ref[...]).astype(BF16)


def _norm_cast(x, g):
    m, d = x.shape
    tm = _pick(m, 512)
    return pl.pallas_call(
        _norm_kernel,
        grid=(m // tm,),
        in_specs=[pl.BlockSpec((tm, d), lambda i: (i, 0)), _const_spec((1, d))],
        out_specs=pl.BlockSpec((tm, d), lambda i: (i, 0)),
        out_shape=jax.ShapeDtypeStruct((m, d), BF16),
        compiler_params=_params(1),
        name="norm_cast",
    )(x, g)


def _mm_kernel(x_ref, w_ref, o_ref):
    o_ref[...] = _dot(x_ref[...], w_ref[...]).astype(o_ref.dtype)


def _matmul(x, w, tm_pref=1024, tn_pref=2304):
    m, k = x.shape
    n = w.shape[1]
    tm = _pick(m, tm_pref)
    tn = _pick(n, tn_pref)
    return pl.pallas_call(
        _mm_kernel,
        grid=(m // tm, n // tn),
        in_specs=[pl.BlockSpec((tm, k), lambda i, j: (i, 0)),
                  pl.BlockSpec((k, tn), lambda i, j: (0, j))],
        out_specs=pl.BlockSpec((tm, tn), lambda i, j: (i, j)),
        out_shape=jax.ShapeDtypeStruct((m, n), BF16),
        compiler_params=_params(2),
        name="matmul",
    )(x, w)


def _mla_prep_kernel(h_ref, wb_ref, gq_ref, gkv_ref, cos_ref, sin_ref, wq_ref, *rest,
                     n_heads, q_lora, kv_lora, nope, rope, scale, expand_kv):
    if expand_kv:
        wuk_ref, wuv_ref, ckv_ref, kr_ref, q_ref, k_ref, v_ref = rest
    else:
        ckv_ref, kr_ref, q_ref = rest
    hn = n_heads * nope
    pb = _dot(h_ref[...], wb_ref[...])
    cq = _rms(pb[:, :q_lora], gq_ref[...])
    ckv = _rms(pb[:, q_lora:q_lora + kv_lora], gkv_ref[...])
    cos = cos_ref[...]
    sin = sin_ref[...]
    o = q_lora + kv_lora
    kr = pb[:, o:o + LANE] * cos + pb[:, o + LANE:o + 2 * LANE] * sin
    ckv_ref[...] = ckv
    kr_ref[...] = kr[:, :rope]
    qall = _dot(cq.astype(BF16), wq_ref[...])
    for hh in range(n_heads):
        c0 = hh * LANE
        qn = qall[:, c0:c0 + nope] * scale
        qr = (qall[:, hn + c0:hn + c0 + LANE] * cos
              + qall[:, 2 * hn + c0:2 * hn + c0 + LANE] * sin) * scale
        q_ref[:, 2 * c0:2 * c0 + nope] = qn.astype(BF16)
        q_ref[:, 2 * c0 + nope:2 * c0 + nope + LANE] = qr.astype(BF16)
    if expand_kv:
        ckvb = ckv.astype(BF16)
        kn = _dot(ckvb, wuk_ref[...])
        v_ref[...] = _dot(ckvb, wuv_ref[...]).astype(BF16)
        krb = kr.astype(BF16)
        for hh in range(n_heads):
            c0 = hh * LANE
            k_ref[:, 2 * c0:2 * c0 + nope] = kn[:, c0:c0 + nope].astype(BF16)
            k_ref[:, 2 * c0 + nope:2 * c0 + nope + LANE] = krb


def _mla_prep(h, wb, gq, gkv, cos_t, sin_t, wq, wuk, wuv, *, seq, n_heads, nope, rope, scale,
              expand_kv):
    m, d = h.shape
    q_lora, kv_lora = gq.shape[1], gkv.shape[1]
    assert nope == LANE and rope <= LANE
    if seq >= MLA_TILE_ROWS:
        tm = _pick(seq, MLA_TILE_ROWS)
        nt = seq // tm
    else:
        tm = seq * _pick(m // seq, MLA_TILE_ROWS // seq)
        nt = 1
        cos_t, sin_t = (jnp.tile(t, (tm // seq, 1)) for t in (cos_t, sin_t))
    hq = n_heads * (nope + LANE)
    row = lambda i: (i, 0)
    tab = lambda i: (i % nt, 0)
    in_specs = [pl.BlockSpec((tm, d), row), _const_spec(wb.shape), _const_spec(gq.shape),
                _const_spec(gkv.shape), pl.BlockSpec((tm, LANE), tab), pl.BlockSpec((tm, LANE), tab),
                _const_spec(wq.shape)]
    args = [h, wb, gq, gkv, cos_t, sin_t, wq]
    out_specs = [pl.BlockSpec((tm, kv_lora), row), pl.BlockSpec((tm, rope), row),
                 pl.BlockSpec((tm, hq), row)]
    out_shape = [jax.ShapeDtypeStruct((m, kv_lora), F32), jax.ShapeDtypeStruct((m, rope), F32),
                 jax.ShapeDtypeStruct((m, hq), BF16)]
    if expand_kv:
        in_specs += [_const_spec(wuk.shape), _const_spec(wuv.shape)]
        args += [wuk, wuv]
        out_specs += [pl.BlockSpec((tm, hq), row), pl.BlockSpec((tm, n_heads * nope), row)]
        out_shape += [jax.ShapeDtypeStruct((m, hq), BF16),
                      jax.ShapeDtypeStruct((m, n_heads * nope), BF16)]
    return pl.pallas_call(
        functools.partial(_mla_prep_kernel, n_heads=n_heads, q_lora=q_lora, kv_lora=kv_lora,
                          nope=nope, rope=rope, scale=scale, expand_kv=expand_kv),
        grid=(m // tm,), in_specs=in_specs, out_specs=out_specs, out_shape=out_shape,
        compiler_params=_params(1), name="mla_prep",
    )(*args)


def _flash_kernel(q_ref, k_ref, v_ref, o_ref, *, tq, heads, dv):
    dk = 2 * LANE
    nq = q_ref.shape[0] // tq
    r = lax.broadcasted_iota(jnp.int32, (tq, tq), 0) // CHUNK
    c = lax.broadcasted_iota(jnp.int32, (tq, tq), 1) // CHUNK
    visible = c <= r
    for qi in range(nq):
        rows = slice(qi * tq, (qi + 1) * tq)
        carry = [(jnp.full((tq, 1), NEG_INF, F32), jnp.zeros((tq, 1), F32),
                  jnp.zeros((tq, dv), F32)) for _ in range(heads)]
        for j in range(qi + 1):
            keys = slice(j * tq, (j + 1) * tq)
            for hh in range(heads):
                m, l, acc = carry[hh]
                s = _dot_nt(q_ref[rows, hh * dk:(hh + 1) * dk],
                            k_ref[keys, hh * dk:(hh + 1) * dk])
                if j == qi:
                    s = jnp.where(visible, s, NEG_INF)
                m_new = jnp.maximum(m, jnp.max(s, axis=-1, keepdims=True))
                p = jnp.exp2(s - m_new)
                alpha = jnp.exp2(m - m_new)
                l = alpha * l + jnp.sum(p, axis=-1, keepdims=True)
                acc = alpha * acc + _dot(p.astype(BF16), v_ref[keys, hh * dv:(hh + 1) * dv])
                carry[hh] = (m_new, l, acc)
        for hh, (_, l, acc) in enumerate(carry):
            o_ref[rows, hh * dv:(hh + 1) * dv] = (acc / l).astype(BF16)


def _flash(q, k, v, *, batch, seq, n_heads, nope):
    m = q.shape[0]
    tq = _pick(seq, 512)
    assert tq % CHUNK == 0
    heads = _pick(n_heads, 2)
    dk = 2 * LANE
    blk = lambda b, h: (b, h)
    return pl.pallas_call(
        functools.partial(_flash_kernel, tq=tq, heads=heads, dv=nope),
        grid=(batch, n_heads // heads),
        in_specs=[pl.BlockSpec((seq, heads * dk), blk), pl.BlockSpec((seq, heads * dk), blk),
                  pl.BlockSpec((seq, heads * nope), blk)],
        out_specs=pl.BlockSpec((seq, heads * nope), blk),
        out_shape=jax.ShapeDtypeStruct((m, n_heads * nope), BF16),
        compiler_params=_params(2), name="flash",
    )(q, k, v)


def _cached_attn_kernel(q_ref, cc_ref, kc_ref, cn_ref, kn_ref, wuk_ref, wuv_ref, o_ref,
                        qlat_scr, qr_scr, *, n_heads, nope, rope, seq):
    ckv_c = cc_ref[...].astype(BF16)
    kr_c = kc_ref[...].astype(BF16)
    ckv_n = cn_ref[...].astype(BF16)
    kr_n = kn_ref[...].astype(BF16)
    for hh in range(n_heads):
        c0 = hh * 2 * LANE
        qlat_scr[hh * seq:(hh + 1) * seq, :] = _dot(q_ref[:, c0:c0 + nope], wuk_ref[hh]).astype(BF16)
        qr_scr[hh * seq:(hh + 1) * seq, :] = q_ref[:, c0 + nope:c0 + nope + rope]
    qlat = qlat_scr[...]
    qr = qr_scr[...]
    s_c = _dot_nt(qlat, ckv_c) + _dot_nt(qr, kr_c)
    s_n = _dot_nt(qlat, ckv_n) + _dot_nt(qr, kr_n)
    mx = jnp.maximum(jnp.max(s_c, axis=-1, keepdims=True), jnp.max(s_n, axis=-1, keepdims=True))
    p_c = jnp.exp(s_c - mx)
    p_n = jnp.exp(s_n - mx)
    den = jnp.sum(p_c, axis=-1, keepdims=True) + jnp.sum(p_n, axis=-1, keepdims=True)
    o_lat = ((_dot(p_c.astype(BF16), ckv_c) + _dot(p_n.astype(BF16), ckv_n)) / den).astype(BF16)
    for hh in range(n_heads):
        o_ref[:, hh * nope:(hh + 1) * nope] = _dot(
            o_lat[hh * seq:(hh + 1) * seq, :], wuv_ref[hh]).astype(BF16)


def _cached_attn(q, cache_ckv, cache_kr, ckv_new, kr_new, wuk_h, wuv_h, *, layer, batch, seq,
                 n_heads, nope, rope):
    past, kv_lora = cache_ckv.shape[2], cache_ckv.shape[3]
    assert seq <= CHUNK and past % CHUNK == 0
    hq = q.shape[1]
    return pl.pallas_call(
        functools.partial(_cached_attn_kernel, n_heads=n_heads, nope=nope, rope=rope, seq=seq),
        grid=(batch,),
        in_specs=[pl.BlockSpec((seq, hq), lambda b: (b, 0)),
                  pl.BlockSpec((None, None, past, kv_lora), lambda b: (layer, b, 0, 0)),
                  pl.BlockSpec((None, None, past, rope), lambda b: (layer, b, 0, 0)),
                  pl.BlockSpec((seq, kv_lora), lambda b: (b, 0)),
                  pl.BlockSpec((seq, rope), lambda b: (b, 0)),
                  _const_spec(wuk_h.shape), _const_spec(wuv_h.shape)],
        out_specs=pl.BlockSpec((seq, n_heads * nope), lambda b: (b, 0)),
        out_shape=jax.ShapeDtypeStruct((batch * seq, n_heads * nope), BF16),
        scratch_shapes=[pltpu.VMEM((n_heads * seq, kv_lora), BF16),
                        pltpu.VMEM((n_heads * seq, rope), BF16)],
        compiler_params=_params(1), name="cached_attn",
    )(q, cache_ckv, cache_kr, ckv_new, kr_new, wuk_h, wuv_h)


def _mixer_kernel(g0_ref, g1_ref, g2_ref, a_ref, gate_ref, u_ref, o_ref, x_ref, hp_ref, hc_ref,
                  bg_ref, wpool_ref, pscale_ref, wdw_ref, bdw_ref, lng_ref, lnb_ref, wpw_ref,
                  wo_ref, wout_ref, gpost_ref, gnext_ref,
                  xo_ref, h_ref, stp_ref, stc_ref, pext, cext, shifted, dw_out, merged_scr,
                  z_scr, ymla_scr, yconv_scr, ypool_scr, act_scr, *, seg, nt, n_tiles, pos0):
    g = pl.program_id(0)
    ti = jnp.minimum(g, n_tiles - 1) % nt
    php, chp, sub = POOL_HIST_PAD, CONV_HIST_PAD, CONV_SUBLANES
    tt, d = x_ref.shape
    n_str = tt // seg

    @pl.when(g == 0)
    def _():
        merged_scr[...] = jnp.zeros_like(merged_scr)

    @pl.when(ti == 0)
    def _():
        pext[:, 0:php, :] = hp_ref[...]
        cext[:, 0:chp, :] = hc_ref[...]

    @pl.when(ti > 0)
    def _():
        pext[:, 0:php, :] = pext[:, seg:seg + php, :]
        cext[:, 0:chp, :] = cext[:, seg:seg + chp, :]

    z_scr[...] = _dot(merged_scr[(g + 1) % 2], wout_ref[...])
    blocks = [slice(r0, r0 + MIX_ROW_BLOCK) for r0 in range(0, tt, MIX_ROW_BLOCK)]
    for rows in blocks:
        xn = x_ref[rows, :] + _rms(z_scr[rows, :], gpost_ref[...])
        xo_ref[rows, :] = xn
        h_ref[rows, :] = _rms(xn, gnext_ref[...]).astype(BF16)

    ymla_scr[...] = _dot(o_ref[...], wo_ref[...])

    glu = a_ref[...].astype(F32) * jax.nn.sigmoid(gate_ref[...].astype(F32))
    off = chp - CONV_HIST
    rb = min(seg, CONV_ROW_BLOCK)
    bias = jnp.broadcast_to(bdw_ref[...], (rb, cext.shape[2]))
    for st in range(n_str):
        cext[st, chp:chp + seg, :] = glu[st * seg:(st + 1) * seg, :]
        whole = cext[st]
        for r in range(1, sub):
            shifted[r - 1, st] = pltpu.roll(whole, chp + seg - r, axis=0)
        for b0 in range(0, seg, rb):
            acc = bias
            for k in range(CONV_K):
                a, r = divmod(off + k, sub)
                lo = b0 + sub * a
                src = cext[st, lo:lo + rb, :] if r == 0 else shifted[r - 1, st, lo:lo + rb, :]
                acc = acc + jnp.tile(wdw_ref[k], (rb // sub, 1)) * src
            dw_out[st * seg + b0:st * seg + b0 + rb, :] = acc
    for rows in blocks:
        acc = dw_out[rows, :]
        mu = jnp.mean(acc, axis=-1, keepdims=True)
        cen = acc - mu
        var = jnp.mean(cen * cen, axis=-1, keepdims=True)
        yc = cen * lax.rsqrt(var + EPS) * lng_ref[...] + lnb_ref[...]
        act_scr[rows, :] = (yc * jax.nn.sigmoid(yc)).astype(BF16)
    yconv_scr[...] = _dot(act_scr[...], wpw_ref[...])
    stc_ref[...] = cext[:, seg:seg + chp, :]

    u_all = u_ref[...].astype(F32)
    for st in range(n_str):
        pext[st, php:php + seg, :] = u_all[st * seg:(st + 1) * seg, :]
    pos = pos0 + ti * seg + lax.broadcasted_iota(jnp.int32, (seg, 1), 0)
    pg, og = wpool_ref.shape[1], wpool_ref.shape[2]
    for gi, w in enumerate(POOL_WINDOWS):
        cnt = jnp.minimum(pos + 1, w).astype(F32)
        deltas = []
        for st in range(n_str):
            tot = pext[st, :, gi * pg:(gi + 1) * pg]
            cur = tot[php:php + seg, :]
            span = 1
            while span < w:
                tot = tot + pltpu.roll(tot, span, axis=0)
                span *= 2
            deltas.append((tot[php:php + seg, :] / cnt - cur).astype(BF16))
        dlt = deltas[0] if n_str == 1 else jnp.concatenate(deltas, axis=0)
        cols = slice(gi * og, (gi + 1) * og)
        ypool_scr[:, cols] = _dot(dlt, wpool_ref[gi]) * pscale_ref[:, cols]
    stp_ref[...] = pext[:, seg:seg + php, :]

    hb = 0.5 * bg_ref[...]
    for rows in blocks:
        total = None
        for br, (g_ref, y_scr) in enumerate(((g0_ref, ypool_scr), (g1_ref, yconv_scr),
                                             (g2_ref, ymla_scr))):
            th = jnp.tanh(0.5 * g_ref[rows, :].astype(F32) + hb[:, br * d:(br + 1) * d])
            y = y_scr[rows, :]
            part = y + th * y
            total = part if total is None else total + part
        merged_scr[g % 2, rows, :] = (0.5 * total).astype(BF16)


def _mixer(pa, o, x, hist_pool, hist_conv, bg, w_pool, pscale, wdw, bdw, lng, lnb, wpw, wo, wout,
           gpost, gnext, *, batch, seq, pos0):
    m, d = x.shape
    g, pg, og = w_pool.shape
    pw = g * pg
    cw = wpw.shape[0]
    col_glu, col_pool = 3 * d, 3 * d + 2 * cw
    assert col_glu % cw == 0 and col_pool % pw == 0
    assert all(w & (w - 1) == 0 and w <= POOL_HIST_PAD for w in POOL_WINDOWS)
    if seq >= MIX_TILE_ROWS:
        seg, n_str = _pick(seq, MIX_TILE_ROWS), 1
    else:
        seg, n_str = seq, _pick(batch, MIX_TILE_ROWS // seq)
    tt = seg * n_str
    nt = seq // seg
    n_tiles = m // tt
    rb = min(seg, CONV_ROW_BLOCK)
    assert seg >= CONV_HIST_PAD and seg >= POOL_HIST_PAD and seg % rb == 0
    assert tt % MIX_ROW_BLOCK == 0
    one = pl.Buffered(1)
    cur = lambda c: (lambda i: (jnp.minimum(i, n_tiles - 1), c))
    prev = lambda i: (jnp.maximum(i - 1, 0), 0)
    stream = lambda i: (jnp.minimum(i, n_tiles - 1) // nt, 0, 0)
    consts = [bg, w_pool, pscale, wdw, bdw, lng, lnb, wpw, wo, wout, gpost, gnext]
    return pl.pallas_call(
        functools.partial(_mixer_kernel, seg=seg, nt=nt, n_tiles=n_tiles, pos0=pos0),
        grid=(n_tiles + 1,),
        in_specs=[pl.BlockSpec((tt, d), cur(0)), pl.BlockSpec((tt, d), cur(1)),
                  pl.BlockSpec((tt, d), cur(2)),
                  pl.BlockSpec((tt, cw), cur(col_glu // cw)),
                  pl.BlockSpec((tt, cw), cur(col_glu // cw + 1)),
                  pl.BlockSpec((tt, pw), cur(col_pool // pw)),
                  pl.BlockSpec((tt, o.shape[1]), cur(0)), pl.BlockSpec((tt, d), prev),
                  pl.BlockSpec((n_str, POOL_HIST_PAD, pw), stream),
                  pl.BlockSpec((n_str, CONV_HIST_PAD, cw), stream)]
                 + [_const_spec(c.shape) for c in consts],
        out_specs=[pl.BlockSpec((tt, d), prev, pipeline_mode=one),
                   pl.BlockSpec((tt, d), prev, pipeline_mode=one),
                   pl.BlockSpec((n_str, POOL_HIST_PAD, pw), stream),
                   pl.BlockSpec((n_str, CONV_HIST_PAD, cw), stream)],
        out_shape=[jax.ShapeDtypeStruct((m, d), F32), jax.ShapeDtypeStruct((m, d), BF16),
                   jax.ShapeDtypeStruct((batch, POOL_HIST_PAD, pw), F32),
                   jax.ShapeDtypeStruct((batch, CONV_HIST_PAD, cw), F32)],
        scratch_shapes=[pltpu.VMEM((n_str, POOL_HIST_PAD + seg, pw), F32),
                        pltpu.VMEM((n_str, CONV_HIST_PAD + seg, cw), F32),
                        pltpu.VMEM((CONV_SUBLANES - 1, n_str, CONV_HIST_PAD + seg, cw), F32),
                        pltpu.VMEM((tt, cw), F32),
                        pltpu.VMEM((2, tt, d), BF16)]
                       + [pltpu.VMEM((tt, d), F32)] * 4 + [pltpu.VMEM((tt, cw), BF16)],
        compiler_params=_params(1), name="mixer",
    )(pa, pa, pa, pa, pa, pa, o, x, hist_pool, hist_conv, *consts)


def _ffn_kernel(*refs, seg, nseg, nt, nk, emit_h):
    h_ref, hist_ref, wup_ref, wdw_ref, bdw_ref, wd_ref, x_ref, gpost_ref = refs[:8]
    rest = refs[8:]
    if emit_h:
        gnext_ref, xo_ref, hn_ref, tail_ref, carry, acc = rest
    else:
        xo_ref, tail_ref, carry, acc = rest
    i = pl.program_id(0)
    kk = pl.program_id(1)
    hp = FFN_HIST_PAD
    tk = wd_ref.shape[0]
    tm = h_ref.shape[0]
    rc = min(tm, FFN_ROW_CHUNK)
    sr = min(seg, rc)
    bias = bdw_ref[...]
    taps = [wdw_ref[k:k + 1, :] for k in range(FFN_K)]

    @pl.when(kk == 0)
    def _():
        acc[...] = jnp.zeros_like(acc)

    ups = [_dot(h_ref[c * rc:(c + 1) * rc, :], wup_ref[...]) for c in range(tm // rc)]
    prev = None
    for c in range(tm // rc):
        rows = slice(c * rc, (c + 1) * rc)
        u = ups[c]
        parts = []
        for p in range(rc // sr):
            r0 = c * rc + p * sr
            s = r0 // seg
            cur = u[p * sr:(p + 1) * sr, :]
            if r0 % seg == 0:
                prev = hist_ref[s]
                if nt > 1:
                    prev = jnp.where((i % nt) == 0, prev, carry[kk])
            ext = jnp.concatenate([prev, cur], axis=0)
            uc = bias + taps[FFN_K - 1] * cur
            for k in range(FFN_K - 1):
                o = hp - FFN_HIST + k
                uc = uc + taps[k] * ext[o:o + sr, :]
            parts.append((jax.nn.gelu(uc[:, :tk], approximate=True) * uc[:, tk:]).astype(BF16))
            prev = cur[sr - hp:, :]
            if (r0 + sr) % seg == 0:
                tail_ref[s] = prev
                if nt > 1:
                    carry[kk] = prev
        f = parts[0] if len(parts) == 1 else jnp.concatenate(parts, axis=0)
        acc[rows, :] += _dot(f, wd_ref[...])

    @pl.when(kk == nk - 1)
    def _():
        xn = x_ref[...] + _rms(acc[...], gpost_ref[...])
        xo_ref[...] = xn
        if emit_h:
            hn_ref[...] = _rms(xn, gnext_ref[...]).astype(BF16)


def _ffn_tile(f):
    return _pick(f, 512)


def _interleave_ffn_cols(a, f):
    tk = _ffn_tile(f)
    lead = a.shape[:-1]
    a = a.reshape(lead + (2, f // tk, tk))
    return jnp.swapaxes(a, -3, -2).reshape(lead + (2 * f,))


def _cast_kernel(src_ref, dst_ref):
    dst_ref[...] = src_ref[...].astype(dst_ref.dtype)


def _interleave_cast_up(w_up, layer, f):
    d = w_up.shape[1]
    tk = _ffn_tile(f)
    nk = f // tk
    return pl.pallas_call(
        _cast_kernel,
        grid=(2 * nk,),
        in_specs=[pl.BlockSpec((None, d, tk), lambda j: (layer, 0, (j % 2) * nk + j // 2))],
        out_specs=pl.BlockSpec((d, tk), lambda j: (0, j)),
        out_shape=jax.ShapeDtypeStruct((d, 2 * f), BF16),
        compiler_params=_params(1), name="cast_up",
    )(w_up)


def _deinterleave_ffn_cols(a, f):
    tk = _ffn_tile(f)
    lead = a.shape[:-1]
    a = a.reshape(lead + (f // tk, 2, tk))
    return jnp.swapaxes(a, -3, -2).reshape(lead + (2 * f,))


def _ffn(h, hist, wup, wdw, bdw, wd, x, gpost, gnext, *, seq):
    m, d = x.shape
    f = wd.shape[0]
    emit_h = gnext is not None
    tm = _pick(m, 1024)
    if tm >= seq:
        assert tm % seq == 0
        seg, nseg, nt = seq, tm // seq, 1
    else:
        assert seq % tm == 0
        seg, nseg, nt = tm, 1, seq // tm
    assert seg >= FFN_HIST_PAD and (seg % FFN_ROW_CHUNK == 0 or FFN_ROW_CHUNK % seg == 0)
    tk = _ffn_tile(f)
    nk = f // tk
    row = lambda i, k: (i, 0)
    col = lambda i, k: (0, k)
    one = pl.Buffered(1)
    in_specs = [pl.BlockSpec((tm, d), row, pipeline_mode=one),
                pl.BlockSpec((nseg, FFN_HIST_PAD, 2 * tk), lambda i, k: (i // nt, 0, k)),
                pl.BlockSpec((d, 2 * tk), col), pl.BlockSpec((FFN_HIST_PAD, 2 * tk), col),
                pl.BlockSpec((1, 2 * tk), col), pl.BlockSpec((tk, d), lambda i, k: (k, 0)),
                pl.BlockSpec((tm, d), row, pipeline_mode=one), _const_spec(gpost.shape)]
    args = [h, hist, wup, wdw, bdw, wd, x, gpost]
    out_specs = [pl.BlockSpec((tm, d), row, pipeline_mode=one)]
    out_shape = [jax.ShapeDtypeStruct((m, d), F32)]
    if emit_h:
        in_specs.append(_const_spec(gnext.shape))
        args.append(gnext)
        out_specs.append(pl.BlockSpec((tm, d), row, pipeline_mode=one))
        out_shape.append(jax.ShapeDtypeStruct((m, d), BF16))
    out_specs.append(pl.BlockSpec((nseg, FFN_HIST_PAD, 2 * tk), lambda i, k: (i, 0, k)))
    out_shape.append(jax.ShapeDtypeStruct((m // seg, FFN_HIST_PAD, 2 * f), F32))
    return pl.pallas_call(
        functools.partial(_ffn_kernel, seg=seg, nseg=nseg, nt=nt, nk=nk, emit_h=emit_h),
        grid=(m // tm, nk), in_specs=in_specs, out_specs=out_specs, out_shape=out_shape,
        scratch_shapes=[pltpu.VMEM((nk, FFN_HIST_PAD, 2 * tk), F32), pltpu.VMEM((tm, d), F32)],
        compiler_params=_params(2), name="ffn",
    )(*args)


def _rotate_half_cols(w):
    half = w.shape[-1] // 2
    return jnp.concatenate([-w[..., half:], w[..., :half]], axis=-1)


def _pad_last(w, n):
    return jnp.pad(w, [(0, 0)] * (w.ndim - 1) + [(0, n - w.shape[-1])])


def _prep_layer(l, dims, w_in, w_uq, w_uk, w_uv, w_o_mla, w_pool, w_conv_dw, w_conv_pw, w_out,
                w_up, w_ffn_dw, w_down):
    d, pw, cw, ql, kl, rope, nope, nh = dims
    o_glu = pw
    o_q = o_glu + 2 * cw
    o_kv = o_q + ql
    o_kr = o_kv + kl
    o_gate = o_kr + rope
    wi = w_in[l]
    wa = jnp.concatenate([wi[:, o_gate:], wi[:, o_glu:o_q], wi[:, :o_glu]], axis=1).astype(BF16)
    wkr = wi[:, o_kr:o_gate]
    wb = jnp.concatenate([wi[:, o_q:o_kr], _pad_last(wkr, LANE),
                          _pad_last(_rotate_half_cols(wkr), LANE)], axis=1).astype(BF16)
    uq = w_uq[l]
    uq_r = uq[..., nope:]
    wq = jnp.concatenate([uq[..., :nope].reshape(ql, nh * nope),
                          _pad_last(uq_r, LANE).reshape(ql, nh * LANE),
                          _pad_last(_rotate_half_cols(uq_r), LANE).reshape(ql, nh * LANE)],
                         axis=1).astype(BF16)
    uk, uv = w_uk[l], w_uv[l]
    return dict(
        wa=wa, wb=wb, wq=wq,
        wuk=uk.reshape(kl, -1).astype(BF16), wuv=uv.reshape(kl, -1).astype(BF16),
        wuk_h=jnp.transpose(uk, (1, 2, 0)).astype(BF16), wuv_h=jnp.transpose(uv, (1, 0, 2)).astype(BF16),
        wo=w_o_mla[l].astype(BF16), wpool=w_pool[l].astype(BF16),
        wdw=_replicate_sublanes(w_conv_dw[l]), wpw=w_conv_pw[l].astype(BF16),
        wout=w_out[l].astype(BF16), wup=_interleave_cast_up(w_up, l, w_down.shape[1]),
        wfdw=_interleave_ffn_cols(_pad_rows(w_ffn_dw[l], FFN_HIST_PAD), w_down.shape[1]),
        wdown=w_down[l].astype(BF16))


def _replicate_sublanes(w):
    return jnp.broadcast_to(w[:, None, :], (w.shape[0], CONV_SUBLANES, w.shape[1]))


def _pad_rows(w, n):
    return jnp.pad(w, [(0, n - w.shape[0])] + [(0, 0)] * (w.ndim - 1))


def _pad_front(s, n):
    return jnp.pad(s, [(0, 0), (n - s.shape[1], 0), (0, 0)])


def _rope_tables(pos0, seq, rope):
    half = rope // 2
    inv = ROPE_THETA ** (-jnp.arange(half, dtype=F32) / half)
    pos = pos0 + jnp.arange(seq, dtype=jnp.int32)
    ang = pos.astype(F32)[:, None] * inv[None, :]
    cos, sin = jnp.cos(ang), jnp.sin(ang)
    return (_pad_last(jnp.concatenate([cos, cos], axis=1), LANE),
            _pad_last(jnp.concatenate([sin, sin], axis=1), LANE))


def _trunk(x3, pos0, cache_ckv, cache_kr, state_pool, state_conv, state_ffn, layers, vecs, dims):
    batch, seq, d = x3.shape
    _, pw, cw, ql, kl, rope, nope, nh = dims
    depth = len(layers)
    m = batch * seq
    x = x3.reshape(m, d)
    cached = cache_ckv is not None
    scale = 1.0 / math.sqrt(nope + rope)
    if not cached:
        scale *= math.log2(math.e)
    cos_t, sin_t = _rope_tables(pos0, seq, rope)
    row = lambda v: v.reshape(1, -1)
    h = _norm_cast(x, row(vecs["g_pre_mix"][0]))
    outs = {k: [] for k in ("ckv", "kr", "pool", "conv", "ffn")}
    for l in range(depth):
        w = layers[l]
        f = w["wdown"].shape[0]
        pa = _matmul(h, w["wa"])
        res = _mla_prep(h, w["wb"], row(vecs["g_q_a"][l]), row(vecs["g_kv_a"][l]), cos_t, sin_t,
                        w["wq"], w["wuk"], w["wuv"], seq=seq, n_heads=nh, nope=nope, rope=rope,
                        scale=scale, expand_kv=not cached)
        if cached:
            ckv, kr, q = res
            o = _cached_attn(q, cache_ckv, cache_kr, ckv, kr, w["wuk_h"], w["wuv_h"], layer=l,
                             batch=batch, seq=seq, n_heads=nh, nope=nope, rope=rope)
            hist_pool = _pad_front(state_pool[l], POOL_HIST_PAD)
            hist_conv = _pad_front(state_conv[l], CONV_HIST_PAD)
            hist_ffn = _interleave_ffn_cols(_pad_front(state_ffn[l], FFN_HIST_PAD), f)
        else:
            ckv, kr, q, kfull, vfull = res
            o = _flash(q, kfull, vfull, batch=batch, seq=seq, n_heads=nh, nope=nope)
            hist_pool = jnp.zeros((batch, POOL_HIST_PAD, pw), F32)
            hist_conv = jnp.zeros((batch, CONV_HIST_PAD, cw), F32)
            hist_ffn = jnp.zeros((batch, FFN_HIST_PAD, w["wfdw"].shape[1]), F32)
        x, h2, st_pool, st_conv = _mixer(
            pa, o, x, hist_pool, hist_conv, row(vecs["b_gate"][l]), w["wpool"],
            row(vecs["pool_scale"][l]), w["wdw"], row(vecs["b_conv_dw"][l]),
            row(vecs["g_conv_ln"][l]), row(vecs["b_conv_ln"][l]), w["wpw"], w["wo"], w["wout"],
            row(vecs["g_post_mix"][l]), row(vecs["g_pre_ffn"][l]), batch=batch, seq=seq, pos0=pos0)
        g_next = row(vecs["g_pre_mix"][l + 1]) if l + 1 < depth else None
        res = _ffn(h2, hist_ffn, w["wup"], w["wfdw"],
                   row(_interleave_ffn_cols(vecs["b_ffn_dw"][l], f)), w["wdown"], x,
                   row(vecs["g_post_ffn"][l]), g_next, seq=seq)
        if g_next is None:
            x, tails = res
        else:
            x, h, tails = res
        outs["ckv"].append(ckv.reshape(batch, seq, kl))
        outs["kr"].append(kr.reshape(batch, seq, rope))
        outs["pool"].append(st_pool[:, POOL_HIST_PAD - POOL_HIST:])
        outs["conv"].append(st_conv[:, CONV_HIST_PAD - CONV_HIST:])
        tails = _deinterleave_ffn_cols(tails, f)
        tails = tails.reshape(batch, -1, FFN_HIST_PAD, 2 * f)[:, -1]
        outs["ffn"].append(tails[:, FFN_HIST_PAD - FFN_HIST:])
    return (x.reshape(batch, seq, d),) + tuple(jnp.stack(outs[k]) for k in ("ckv", "kr", "pool", "conv", "ffn"))


def kernel(x_prompt, x_sample, cache_ckv, cache_krope, state_pool, state_conv, state_ffn,
           g_pre_mix, w_in, b_gate, g_q_a, g_kv_a, w_uq, w_uk, w_uv, w_o_mla,
           w_pool, pool_scale, w_conv_dw, b_conv_dw, g_conv_ln, b_conv_ln, w_conv_pw,
           w_out, g_post_mix, g_pre_ffn, w_up, w_ffn_dw, b_ffn_dw, w_down, g_post_ffn):
    depth, d = g_pre_mix.shape
    nh = w_uq.shape[2]
    nope = w_uk.shape[3]
    rope = w_uq.shape[3] - nope
    dims = (d, state_pool.shape[3], state_conv.shape[3], w_uq.shape[1], w_uk.shape[1], rope, nope, nh)
    layers = [_prep_layer(l, dims, w_in, w_uq, w_uk, w_uv, w_o_mla, w_pool, w_conv_dw, w_conv_pw,
                          w_out, w_up, w_ffn_dw, w_down) for l in range(depth)]
    vecs = dict(g_pre_mix=g_pre_mix, b_gate=b_gate, g_q_a=g_q_a, g_kv_a=g_kv_a,
                pool_scale=pool_scale, b_conv_dw=b_conv_dw, g_conv_ln=g_conv_ln,
                b_conv_ln=b_conv_ln, g_post_mix=g_post_mix, g_pre_ffn=g_pre_ffn,
                b_ffn_dw=b_ffn_dw, g_post_ffn=g_post_ffn)
    prompt = _trunk(x_prompt, 0, None, None, None, None, None, layers, vecs, dims)
    sample = _trunk(x_sample, cache_ckv.shape[2], cache_ckv, cache_krope, state_pool, state_conv,
                    state_ffn, layers, vecs, dims)
    return (prompt[0], sample[0]) + prompt[1:] + sample[1:]
```

```python
import functools
import math

import jax
import jax.numpy as jnp
from jax import lax
from jax.experimental import pallas as pl
from jax.experimental.pallas import tpu as pltpu

CHUNK = 64
EPS = 1e-6
NEG_INF = -1e30
POOL_WINDOWS = (2, 4, 8, 16)
POOL_HIST = max(POOL_WINDOWS) - 1
POOL_HIST_PAD = 16
CONV_K = 31
CONV_HIST = CONV_K - 1
CONV_HIST_PAD = 32
CONV_SUBLANES = 8
CONV_ROW_BLOCK = 32
MIX_ROW_BLOCK = 16
MIX_TILE_ROWS = 256
MLA_TILE_ROWS = 512
FFN_K = 3
FFN_HIST = FFN_K - 1
FFN_HIST_PAD = 8
FFN_ROW_CHUNK = 512
ROPE_THETA = 10000.0
LANE = 128
V7X_VMEM_LIMIT = 60 * 1024 * 1024

F32 = jnp.float32
BF16 = jnp.bfloat16


def _params(n_axes):
    return pltpu.CompilerParams(dimension_semantics=("arbitrary",) * n_axes,
                                vmem_limit_bytes=V7X_VMEM_LIMIT)


def _const_spec(shape):
    zeros = (0,) * len(shape)
    return pl.BlockSpec(shape, lambda *_: zeros, pipeline_mode=pl.Buffered(1))


def _dot(a, b):
    return jnp.dot(a, b, preferred_element_type=F32)


def _dot_nt(a, b):
    return lax.dot_general(a, b, (((1,), (1,)), ((), ())), preferred_element_type=F32)


def _rms(x, g):
    return x * lax.rsqrt(jnp.mean(x * x, axis=-1, keepdims=True) + EPS) * g


def _pick(n, pref):
    t = min(n, pref)
    while n % t:
        t -= 1
    return t


def _norm_kernel(x_ref, g_ref, h_ref):
    h_ref[...] = _rms(x_ref[...], g_ref[...]).astype(BF16)


def _norm_cast(x, g):
    m, d = x.shape
    tm = _pick(m, 512)
    return pl.pallas_call(
        _norm_kernel,
        grid=(m // tm,),
        in_specs=[pl.BlockSpec((tm, d), lambda i: (i, 0)), _const_spec((1, d))],
        out_specs=pl.BlockSpec((tm, d), lambda i: (i, 0)),
        out_shape=jax.ShapeDtypeStruct((m, d), BF16),
        compiler_params=_params(1),
        name="norm_cast",
    )(x, g)


def _mm_kernel(x_ref, w_ref, o_ref):
    o_ref[...] = _dot(x_ref[...], w_ref[...]).astype(o_ref.dtype)


def _matmul(x, w, tm_pref=1024, tn_pref=2304):
    m, k = x.shape
    n = w.shape[1]
    tm = _pick(m, tm_pref)
    tn = _pick(n, tn_pref)
    return pl.pallas_call(
        _mm_kernel,
        grid=(m // tm, n // tn),
        in_specs=[pl.BlockSpec((tm, k), lambda i, j: (i, 0)),
                  pl.BlockSpec((k, tn), lambda i, j: (0, j))],
        out_specs=pl.BlockSpec((tm, tn), lambda i, j: (i, j)),
        out_shape=jax.ShapeDtypeStruct((m, n), BF16),
        compiler_params=_params(2),
        name="matmul",
    )(x, w)


def _mla_prep_kernel(h_ref, wb_ref, gq_ref, gkv_ref, cos_ref, sin_ref, wq_ref, *rest,
                     n_heads, q_lora, kv_lora, nope, rope, scale, expand_kv):
    if expand_kv:
        wuk_ref, wuv_ref, ckv_ref, kr_ref, q_ref, k_ref, v_ref = rest
    else:
        ckv_ref, kr_ref, q_ref = rest
    hn = n_heads * nope
    pb = _dot(h_ref[...], wb_ref[...])
    cq = _rms(pb[:, :q_lora], gq_ref[...])
    ckv = _rms(pb[:, q_lora:q_lora + kv_lora], gkv_ref[...])
    cos = cos_ref[...]
    sin = sin_ref[...]
    o = q_lora + kv_lora
    kr = pb[:, o:o + LANE] * cos + pb[:, o + LANE:o + 2 * LANE] * sin
    ckv_ref[...] = ckv
    kr_ref[...] = kr[:, :rope]
    qall = _dot(cq.astype(BF16), wq_ref[...])
    for hh in range(n_heads):
        c0 = hh * LANE
        qn = qall[:, c0:c0 + nope] * scale
        qr = (qall[:, hn + c0:hn + c0 + LANE] * cos
              + qall[:, 2 * hn + c0:2 * hn + c0 + LANE] * sin) * scale
        q_ref[:, 2 * c0:2 * c0 + nope] = qn.astype(BF16)
        q_ref[:, 2 * c0 + nope:2 * c0 + nope + LANE] = qr.astype(BF16)
    if expand_kv:
        ckvb = ckv.astype(BF16)
        kn = _dot(ckvb, wuk_ref[...])
        v_ref[...] = _dot(ckvb, wuv_ref[...]).astype(BF16)
        krb = kr.astype(BF16)
        for hh in range(n_heads):
            c0 = hh * LANE
            k_ref[:, 2 * c0:2 * c0 + nope] = kn[:, c0:c0 + nope].astype(BF16)
            k_ref[:, 2 * c0 + nope:2 * c0 + nope + LANE] = krb


def _mla_prep(h, wb, gq, gkv, cos_t, sin_t, wq, wuk, wuv, *, seq, n_heads, nope, rope, scale,
              expand_kv):
    m, d = h.shape
    q_lora, kv_lora = gq.shape[1], gkv.shape[1]
    assert nope == LANE and rope <= LANE
    if seq >= MLA_TILE_ROWS:
        tm = _pick(seq, MLA_TILE_ROWS)
        nt = seq // tm
    else:
        tm = seq * _pick(m // seq, MLA_TILE_ROWS // seq)
        nt = 1
        cos_t, sin_t = (jnp.tile(t, (tm // seq, 1)) for t in (cos_t, sin_t))
    hq = n_heads * (nope + LANE)
    row = lambda i: (i, 0)
    tab = lambda i: (i % nt, 0)
    in_specs = [pl.BlockSpec((tm, d), row), _const_spec(wb.shape), _const_spec(gq.shape),
                _const_spec(gkv.shape), pl.BlockSpec((tm, LANE), tab), pl.BlockSpec((tm, LANE), tab),
                _const_spec(wq.shape)]
    args = [h, wb, gq, gkv, cos_t, sin_t, wq]
    out_specs = [pl.BlockSpec((tm, kv_lora), row), pl.BlockSpec((tm, rope), row),
                 pl.BlockSpec((tm, hq), row)]
    out_shape = [jax.ShapeDtypeStruct((m, kv_lora), F32), jax.ShapeDtypeStruct((m, rope), F32),
                 jax.ShapeDtypeStruct((m, hq), BF16)]
    if expand_kv:
        in_specs += [_const_spec(wuk.shape), _const_spec(wuv.shape)]
        args += [wuk, wuv]
        out_specs += [pl.BlockSpec((tm, hq), row), pl.BlockSpec((tm, n_heads * nope), row)]
        out_shape += [jax.ShapeDtypeStruct((m, hq), BF16),
                      jax.ShapeDtypeStruct((m, n_heads * nope), BF16)]
    return pl.pallas_call(
        functools.partial(_mla_prep_kernel, n_heads=n_heads, q_lora=q_lora, kv_lora=kv_lora,
                          nope=nope, rope=rope, scale=scale, expand_kv=expand_kv),
        grid=(m // tm,), in_specs=in_specs, out_specs=out_specs, out_shape=out_shape,
        compiler_params=_params(1), name="mla_prep",
    )(*args)


def _flash_kernel(q_ref, k_ref, v_ref, o_ref, *, tq, heads, dv):
    dk = 2 * LANE
    nq = q_ref.shape[0] // tq
    r = lax.broadcasted_iota(jnp.int32, (tq, tq), 0) // CHUNK
    c = lax.broadcasted_iota(jnp.int32, (tq, tq), 1) // CHUNK
    visible = c <= r
    for qi in range(nq):
        rows = slice(qi * tq, (qi + 1) * tq)
        carry = [(jnp.full((tq, 1), NEG_INF, F32), jnp.zeros((tq, 1), F32),
                  jnp.zeros((tq, dv), F32)) for _ in range(heads)]
        for j in range(qi + 1):
            keys = slice(j * tq, (j + 1) * tq)
            for hh in range(heads):
                m, l, acc = carry[hh]
                s = _dot_nt(q_ref[rows, hh * dk:(hh + 1) * dk],
                            k_ref[keys, hh * dk:(hh + 1) * dk])
                if j == qi:
                    s = jnp.where(visible, s, NEG_INF)
                m_new = jnp.maximum(m, jnp.max(s, axis=-1, keepdims=True))
                p = jnp.exp2(s - m_new)
                alpha = jnp.exp2(m - m_new)
                l = alpha * l + jnp.sum(p, axis=-1, keepdims=True)
                acc = alpha * acc + _dot(p.astype(BF16), v_ref[keys, hh * dv:(hh + 1) * dv])
                carry[hh] = (m_new, l, acc)
        for hh, (_, l, acc) in enumerate(carry):
            o_ref[rows, hh * dv:(hh + 1) * dv] = (acc / l).astype(BF16)


def _flash(q, k, v, *, batch, seq, n_heads, nope):
    m = q.shape[0]
    tq = _pick(seq, 256)
    assert tq % CHUNK == 0
    heads = _pick(n_heads, 2)
    dk = 2 * LANE
    blk = lambda b, h: (b, h)
    return pl.pallas_call(
        functools.partial(_flash_kernel, tq=tq, heads=heads, dv=nope),
        grid=(batch, n_heads // heads),
        in_specs=[pl.BlockSpec((seq, heads * dk), blk), pl.BlockSpec((seq, heads * dk), blk),
                  pl.BlockSpec((seq, heads * nope), blk)],
        out_specs=pl.BlockSpec((seq, heads * nope), blk),
        out_shape=jax.ShapeDtypeStruct((m, n_heads * nope), BF16),
        compiler_params=_params(2), name="flash",
    )(q, k, v)


def _cached_attn_kernel(q_ref, cc_ref, kc_ref, cn_ref, kn_ref, wuk_ref, wuv_ref, o_ref,
                        qlat_scr, qr_scr, *, n_heads, nope, rope, seq):
    ckv_c = cc_ref[...].astype(BF16)
    kr_c = kc_ref[...].astype(BF16)
    ckv_n = cn_ref[...].astype(BF16)
    kr_n = kn_ref[...].astype(BF16)
    for hh in range(n_heads):
        c0 = hh * 2 * LANE
        qlat_scr[hh * seq:(hh + 1) * seq, :] = _dot(q_ref[:, c0:c0 + nope], wuk_ref[hh]).astype(BF16)
        qr_scr[hh * seq:(hh + 1) * seq, :] = q_ref[:, c0 + nope:c0 + nope + rope]
    qlat = qlat_scr[...]
    qr = qr_scr[...]
    s_c = _dot_nt(qlat, ckv_c) + _dot_nt(qr, kr_c)
    s_n = _dot_nt(qlat, ckv_n) + _dot_nt(qr, kr_n)
    mx = jnp.maximum(jnp.max(s_c, axis=-1, keepdims=True), jnp.max(s_n, axis=-1, keepdims=True))
    p_c = jnp.exp(s_c - mx)
    p_n = jnp.exp(s_n - mx)
    den = jnp.sum(p_c, axis=-1, keepdims=True) + jnp.sum(p_n, axis=-1, keepdims=True)
    o_lat = ((_dot(p_c.astype(BF16), ckv_c) + _dot(p_n.astype(BF16), ckv_n)) / den).astype(BF16)
    for hh in range(n_heads):
        o_ref[:, hh * nope:(hh + 1) * nope] = _dot(
            o_lat[hh * seq:(hh + 1) * seq, :], wuv_ref[hh]).astype(BF16)


def _cached_attn(q, cache_ckv, cache_kr, ckv_new, kr_new, wuk_h, wuv_h, *, layer, batch, seq,
                 n_heads, nope, rope):
    past, kv_lora = cache_ckv.shape[2], cache_ckv.shape[3]
    assert seq <= CHUNK and past % CHUNK == 0
    hq = q.shape[1]
    return pl.pallas_call(
        functools.partial(_cached_attn_kernel, n_heads=n_heads, nope=nope, rope=rope, seq=seq),
        grid=(batch,),
        in_specs=[pl.BlockSpec((seq, hq), lambda b: (b, 0)),
                  pl.BlockSpec((None, None, past, kv_lora), lambda b: (layer, b, 0, 0)),
                  pl.BlockSpec((None, None, past, rope), lambda b: (layer, b, 0, 0)),
                  pl.BlockSpec((seq, kv_lora), lambda b: (b, 0)),
                  pl.BlockSpec((seq, rope), lambda b: (b, 0)),
                  _const_spec(wuk_h.shape), _const_spec(wuv_h.shape)],
        out_specs=pl.BlockSpec((seq, n_heads * nope), lambda b: (b, 0)),
        out_shape=jax.ShapeDtypeStruct((batch * seq, n_heads * nope), BF16),
        scratch_shapes=[pltpu.VMEM((n_heads * seq, kv_lora), BF16),
                        pltpu.VMEM((n_heads * seq, rope), BF16)],
        compiler_params=_params(1), name="cached_attn",
    )(q, cache_ckv, cache_kr, ckv_new, kr_new, wuk_h, wuv_h)


def _mixer_kernel(g0_ref, g1_ref, g2_ref, a_ref, gate_ref, u_ref, o_ref, x_ref, hp_ref, hc_ref,
                  bg_ref, wpool_ref, pscale_ref, wdw_ref, bdw_ref, lng_ref, lnb_ref, wpw_ref,
                  wo_ref, wout_ref, gpost_ref, gnext_ref,
                  xo_ref, h_ref, stp_ref, stc_ref, pext, cext, shifted, dw_out, merged_scr,
                  z_scr, ymla_scr, yconv_scr, ypool_scr, act_scr, *, seg, nt, n_tiles, pos0):
    g = pl.program_id(0)
    ti = jnp.minimum(g, n_tiles - 1) % nt
    php, chp, sub = POOL_HIST_PAD, CONV_HIST_PAD, CONV_SUBLANES
    tt, d = x_ref.shape
    n_str = tt // seg

    @pl.when(g == 0)
    def _():
        merged_scr[...] = jnp.zeros_like(merged_scr)

    @pl.when(ti == 0)
    def _():
        pext[:, 0:php, :] = hp_ref[...]
        cext[:, 0:chp, :] = hc_ref[...]

    @pl.when(ti > 0)
    def _():
        pext[:, 0:php, :] = pext[:, seg:seg + php, :]
        cext[:, 0:chp, :] = cext[:, seg:seg + chp, :]

    z_scr[...] = _dot(merged_scr[(g + 1) % 2], wout_ref[...])
    blocks = [slice(r0, r0 + MIX_ROW_BLOCK) for r0 in range(0, tt, MIX_ROW_BLOCK)]
    for rows in blocks:
        xn = x_ref[rows, :] + _rms(z_scr[rows, :], gpost_ref[...])
        xo_ref[rows, :] = xn
        h_ref[rows, :] = _rms(xn, gnext_ref[...]).astype(BF16)

    ymla_scr[...] = _dot(o_ref[...], wo_ref[...])

    glu = a_ref[...].astype(F32) * jax.nn.sigmoid(gate_ref[...].astype(F32))
    off = chp - CONV_HIST
    rb = min(seg, CONV_ROW_BLOCK)
    bias = jnp.broadcast_to(bdw_ref[...], (rb, cext.shape[2]))
    for st in range(n_str):
        cext[st, chp:chp + seg, :] = glu[st * seg:(st + 1) * seg, :]
        whole = cext[st]
        for r in range(1, sub):
            shifted[r - 1, st] = pltpu.roll(whole, chp + seg - r, axis=0)
        for b0 in range(0, seg, rb):
            acc = bias
            for k in range(CONV_K):
                a, r = divmod(off + k, sub)
                lo = b0 + sub * a
                src = cext[st, lo:lo + rb, :] if r == 0 else shifted[r - 1, st, lo:lo + rb, :]
                acc = acc + jnp.tile(wdw_ref[k], (rb // sub, 1)) * src
            dw_out[st * seg + b0:st * seg + b0 + rb, :] = acc
    for rows in blocks:
        acc = dw_out[rows, :]
        mu = jnp.mean(acc, axis=-1, keepdims=True)
        cen = acc - mu
        var = jnp.mean(cen * cen, axis=-1, keepdims=True)
        yc = cen * lax.rsqrt(var + EPS) * lng_ref[...] + lnb_ref[...]
        act_scr[rows, :] = (yc * jax.nn.sigmoid(yc)).astype(BF16)
    yconv_scr[...] = _dot(act_scr[...], wpw_ref[...])
    stc_ref[...] = cext[:, seg:seg + chp, :]

    u_all = u_ref[...].astype(F32)
    for st in range(n_str):
        pext[st, php:php + seg, :] = u_all[st * seg:(st + 1) * seg, :]
    pos = pos0 + ti * seg + lax.broadcasted_iota(jnp.int32, (seg, 1), 0)
    pg, og = wpool_ref.shape[1], wpool_ref.shape[2]
    for gi, w in enumerate(POOL_WINDOWS):
        cnt = jnp.minimum(pos + 1, w).astype(F32)
        deltas = []
        for st in range(n_str):
            tot = pext[st, :, gi * pg:(gi + 1) * pg]
            cur = tot[php:php + seg, :]
            span = 1
            while span < w:
                tot = tot + pltpu.roll(tot, span, axis=0)
                span *= 2
            deltas.append((tot[php:php + seg, :] / cnt - cur).astype(BF16))
        dlt = deltas[0] if n_str == 1 else jnp.concatenate(deltas, axis=0)
        cols = slice(gi * og, (gi + 1) * og)
        ypool_scr[:, cols] = _dot(dlt, wpool_ref[gi]) * pscale_ref[:, cols]
    stp_ref[...] = pext[:, seg:seg + php, :]

    hb = 0.5 * bg_ref[...]
    for rows in blocks:
        total = None
        for br, (g_ref, y_scr) in enumerate(((g0_ref, ypool_scr), (g1_ref, yconv_scr),
                                             (g2_ref, ymla_scr))):
            th = jnp.tanh(0.5 * g_ref[rows, :].astype(F32) + hb[:, br * d:(br + 1) * d])
            y = y_scr[rows, :]
            part = y + th * y
            total = part if total is None else total + part
        merged_scr[g % 2, rows, :] = (0.5 * total).astype(BF16)


def _mixer(pa, o, x, hist_pool, hist_conv, bg, w_pool, pscale, wdw, bdw, lng, lnb, wpw, wo, wout,
           gpost, gnext, *, batch, seq, pos0):
    m, d = x.shape
    g, pg, og = w_pool.shape
    pw = g * pg
    cw = wpw.shape[0]
    col_glu, col_pool = 3 * d, 3 * d + 2 * cw
    assert col_glu % cw == 0 and col_pool % pw == 0
    assert all(w & (w - 1) == 0 and w <= POOL_HIST_PAD for w in POOL_WINDOWS)
    if seq >= MIX_TILE_ROWS:
        seg, n_str = _pick(seq, MIX_TILE_ROWS), 1
    else:
        seg, n_str = seq, _pick(batch, MIX_TILE_ROWS // seq)
    tt = seg * n_str
    nt = seq // seg
    n_tiles = m // tt
    rb = min(seg, CONV_ROW_BLOCK)
    assert seg >= CONV_HIST_PAD and seg >= POOL_HIST_PAD and seg % rb == 0
    assert tt % MIX_ROW_BLOCK == 0
    one = pl.Buffered(1)
    cur = lambda c: (lambda i: (jnp.minimum(i, n_tiles - 1), c))
    prev = lambda i: (jnp.maximum(i - 1, 0), 0)
    stream = lambda i: (jnp.minimum(i, n_tiles - 1) // nt, 0, 0)
    consts = [bg, w_pool, pscale, wdw, bdw, lng, lnb, wpw, wo, wout, gpost, gnext]
    return pl.pallas_call(
        functools.partial(_mixer_kernel, seg=seg, nt=nt, n_tiles=n_tiles, pos0=pos0),
        grid=(n_tiles + 1,),
        in_specs=[pl.BlockSpec((tt, d), cur(0)), pl.BlockSpec((tt, d), cur(1)),
                  pl.BlockSpec((tt, d), cur(2)),
                  pl.BlockSpec((tt, cw), cur(col_glu // cw)),
                  pl.BlockSpec((tt, cw), cur(col_glu // cw + 1)),
                  pl.BlockSpec((tt, pw), cur(col_pool // pw)),
                  pl.BlockSpec((tt, o.shape[1]), cur(0)), pl.BlockSpec((tt, d), prev),
                  pl.BlockSpec((n_str, POOL_HIST_PAD, pw), stream),
                  pl.BlockSpec((n_str, CONV_HIST_PAD, cw), stream)]
                 + [_const_spec(c.shape) for c in consts],
        out_specs=[pl.BlockSpec((tt, d), prev, pipeline_mode=one),
                   pl.BlockSpec((tt, d), prev, pipeline_mode=one),
                   pl.BlockSpec((n_str, POOL_HIST_PAD, pw), stream),
                   pl.BlockSpec((n_str, CONV_HIST_PAD, cw), stream)],
        out_shape=[jax.ShapeDtypeStruct((m, d), F32), jax.ShapeDtypeStruct((m, d), BF16),
                   jax.ShapeDtypeStruct((batch, POOL_HIST_PAD, pw), F32),
                   jax.ShapeDtypeStruct((batch, CONV_HIST_PAD, cw), F32)],
        scratch_shapes=[pltpu.VMEM((n_str, POOL_HIST_PAD + seg, pw), F32),
                        pltpu.VMEM((n_str, CONV_HIST_PAD + seg, cw), F32),
                        pltpu.VMEM((CONV_SUBLANES - 1, n_str, CONV_HIST_PAD + seg, cw), F32),
                        pltpu.VMEM((tt, cw), F32),
                        pltpu.VMEM((2, tt, d), BF16)]
                       + [pltpu.VMEM((tt, d), F32)] * 4 + [pltpu.VMEM((tt, cw), BF16)],
        compiler_params=_params(1), name="mixer",
    )(pa, pa, pa, pa, pa, pa, o, x, hist_pool, hist_conv, *consts)


def _ffn_kernel(*refs, seg, nseg, nt, nk, emit_h):
    h_ref, hist_ref, wup_ref, wdw_ref, bdw_ref, wd_ref, x_ref, gpost_ref = refs[:8]
    rest = refs[8:]
    if emit_h:
        gnext_ref, xo_ref, hn_ref, tail_ref, carry, acc = rest
    else:
        xo_ref, tail_ref, carry, acc = rest
    i = pl.program_id(0)
    kk = pl.program_id(1)
    hp = FFN_HIST_PAD
    tk = wd_ref.shape[0]
    tm = h_ref.shape[0]
    rc = min(tm, FFN_ROW_CHUNK)
    sr = min(seg, rc)
    bias = bdw_ref[...]
    taps = [wdw_ref[k:k + 1, :] for k in range(FFN_K)]

    @pl.when(kk == 0)
    def _():
        acc[...] = jnp.zeros_like(acc)

    ups = [_dot(h_ref[c * rc:(c + 1) * rc, :], wup_ref[...]) for c in range(tm // rc)]
    prev = None
    for c in range(tm // rc):
        rows = slice(c * rc, (c + 1) * rc)
        u = ups[c]
        parts = []
        for p in range(rc // sr):
            r0 = c * rc + p * sr
            s = r0 // seg
            cur = u[p * sr:(p + 1) * sr, :]
            if r0 % seg == 0:
                prev = hist_ref[s]
                if nt > 1:
                    prev = jnp.where((i % nt) == 0, prev, carry[kk])
            ext = jnp.concatenate([prev, cur], axis=0)
            uc = bias + taps[FFN_K - 1] * cur
            for k in range(FFN_K - 1):
                o = hp - FFN_HIST + k
                uc = uc + taps[k] * ext[o:o + sr, :]
            parts.append((jax.nn.gelu(uc[:, :tk], approximate=True) * uc[:, tk:]).astype(BF16))
            prev = cur[sr - hp:, :]
            if (r0 + sr) % seg == 0:
                tail_ref[s] = prev
                if nt > 1:
                    carry[kk] = prev
        f = parts[0] if len(parts) == 1 else jnp.concatenate(parts, axis=0)
        acc[rows, :] += _dot(f, wd_ref[...])

    @pl.when(kk == nk - 1)
    def _():
        xn = x_ref[...] + _rms(acc[...], gpost_ref[...])
        xo_ref[...] = xn
        if emit_h:
            hn_ref[...] = _rms(xn, gnext_ref[...]).astype(BF16)


def _ffn_tile(f):
    return _pick(f, 512)


def _interleave_ffn_cols(a, f):
    tk = _ffn_tile(f)
    lead = a.shape[:-1]
    a = a.reshape(lead + (2, f // tk, tk))
    return jnp.swapaxes(a, -3, -2).reshape(lead + (2 * f,))


def _cast_kernel(src_ref, dst_ref):
    dst_ref[...] = src_ref[...].astype(dst_ref.dtype)


def _interleave_cast_up(w_up, layer, f):
    d = w_up.shape[1]
    tk = _ffn_tile(f)
    nk = f // tk
    return pl.pallas_call(
        _cast_kernel,
        grid=(2 * nk,),
        in_specs=[pl.BlockSpec((None, d, tk), lambda j: (layer, 0, (j % 2) * nk + j // 2))],
        out_specs=pl.BlockSpec((d, tk), lambda j: (0, j)),
        out_shape=jax.ShapeDtypeStruct((d, 2 * f), BF16),
        compiler_params=_params(1), name="cast_up",
    )(w_up)


def _deinterleave_ffn_cols(a, f):
    tk = _ffn_tile(f)
    lead = a.shape[:-1]
    a = a.reshape(lead + (f // tk, 2, tk))
    return jnp.swapaxes(a, -3, -2).reshape(lead + (2 * f,))


def _ffn(h, hist, wup, wdw, bdw, wd, x, gpost, gnext, *, seq):
    m, d = x.shape
    f = wd.shape[0]
    emit_h = gnext is not None
    tm = _pick(m, 1024)
    if tm >= seq:
        assert tm % seq == 0
        seg, nseg, nt = seq, tm // seq, 1
    else:
        assert seq % tm == 0
        seg, nseg, nt = tm, 1, seq // tm
    assert seg >= FFN_HIST_PAD and (seg % FFN_ROW_CHUNK == 0 or FFN_ROW_CHUNK % seg == 0)
    tk = _ffn_tile(f)
    nk = f // tk
    row = lambda i, k: (i, 0)
    col = lambda i, k: (0, k)
    one = pl.Buffered(1)
    in_specs = [pl.BlockSpec((tm, d), row, pipeline_mode=one),
                pl.BlockSpec((nseg, FFN_HIST_PAD, 2 * tk), lambda i, k: (i // nt, 0, k)),
                pl.BlockSpec((d, 2 * tk), col), pl.BlockSpec((FFN_HIST_PAD, 2 * tk), col),
                pl.BlockSpec((1, 2 * tk), col), pl.BlockSpec((tk, d), lambda i, k: (k, 0)),
                pl.BlockSpec((tm, d), row, pipeline_mode=one), _const_spec(gpost.shape)]
    args = [h, hist, wup, wdw, bdw, wd, x, gpost]
    out_specs = [pl.BlockSpec((tm, d), row, pipeline_mode=one)]
    out_shape = [jax.ShapeDtypeStruct((m, d), F32)]
    if emit_h:
        in_specs.append(_const_spec(gnext.shape))
        args.append(gnext)
        out_specs.append(pl.BlockSpec((tm, d), row, pipeline_mode=one))
        out_shape.append(jax.ShapeDtypeStruct((m, d), BF16))
    out_specs.append(pl.BlockSpec((nseg, FFN_HIST_PAD, 2 * tk), lambda i, k: (i, 0, k)))
    out_shape.append(jax.ShapeDtypeStruct((m // seg, FFN_HIST_PAD, 2 * f), F32))
    return pl.pallas_call(
        functools.partial(_ffn_kernel, seg=seg, nseg=nseg, nt=nt, nk=nk, emit_h=emit_h),
        grid=(m // tm, nk), in_specs=in_specs, out_specs=out_specs, out_shape=out_shape,
        scratch_shapes=[pltpu.VMEM((nk, FFN_HIST_PAD, 2 * tk), F32), pltpu.VMEM((tm, d), F32)],
        compiler_params=_params(2), name="ffn",
    )(*args)


def _rotate_half_cols(w):
    half = w.shape[-1] // 2
    return jnp.concatenate([-w[..., half:], w[..., :half]], axis=-1)


def _pad_last(w, n):
    return jnp.pad(w, [(0, 0)] * (w.ndim - 1) + [(0, n - w.shape[-1])])


def _prep_layer(l, dims, w_in, w_uq, w_uk, w_uv, w_o_mla, w_pool, w_conv_dw, w_conv_pw, w_out,
                w_up, w_ffn_dw, w_down):
    d, pw, cw, ql, kl, rope, nope, nh = dims
    o_glu = pw
    o_q = o_glu + 2 * cw
    o_kv = o_q + ql
    o_kr = o_kv + kl
    o_gate = o_kr + rope
    wi = w_in[l]
    wa = jnp.concatenate([wi[:, o_gate:], wi[:, o_glu:o_q], wi[:, :o_glu]], axis=1).astype(BF16)
    wkr = wi[:, o_kr:o_gate]
    wb = jnp.concatenate([wi[:, o_q:o_kr], _pad_last(wkr, LANE),
                          _pad_last(_rotate_half_cols(wkr), LANE)], axis=1).astype(BF16)
    uq = w_uq[l]
    uq_r = uq[..., nope:]
    wq = jnp.concatenate([uq[..., :nope].reshape(ql, nh * nope),
                          _pad_last(uq_r, LANE).reshape(ql, nh * LANE),
                          _pad_last(_rotate_half_cols(uq_r), LANE).reshape(ql, nh * LANE)],
                         axis=1).astype(BF16)
    uk, uv = w_uk[l], w_uv[l]
    return dict(
        wa=wa, wb=wb, wq=wq,
        wuk=uk.reshape(kl, -1).astype(BF16), wuv=uv.reshape(kl, -1).astype(BF16),
        wuk_h=jnp.transpose(uk, (1, 2, 0)).astype(BF16), wuv_h=jnp.transpose(uv, (1, 0, 2)).astype(BF16),
        wo=w_o_mla[l].astype(BF16), wpool=w_pool[l].astype(BF16),
        wdw=_replicate_sublanes(w_conv_dw[l]), wpw=w_conv_pw[l].astype(BF16),
        wout=w_out[l].astype(BF16), wup=_interleave_cast_up(w_up, l, w_down.shape[1]),
        wfdw=_interleave_ffn_cols(_pad_rows(w_ffn_dw[l], FFN_HIST_PAD), w_down.shape[1]),
        wdown=w_down[l].astype(BF16))


def _replicate_sublanes(w):
    return jnp.broadcast_to(w[:, None, :], (w.shape[0], CONV_SUBLANES, w.shape[1]))


def _pad_rows(w, n):
    return jnp.pad(w, [(0, n - w.shape[0])] + [(0, 0)] * (w.ndim - 1))


def _pad_front(s, n):
    return jnp.pad(s, [(0, 0), (n - s.shape[1], 0), (0, 0)])


def _rope_tables(pos0, seq, rope):
    half = rope // 2
    inv = ROPE_THETA ** (-jnp.arange(half, dtype=F32) / half)
    pos = pos0 + jnp.arange(seq, dtype=jnp.int32)
    ang = pos.astype(F32)[:, None] * inv[None, :]
    cos, sin = jnp.cos(ang), jnp.sin(ang)
    return (_pad_last(jnp.concatenate([cos, cos], axis=1), LANE),
            _pad_last(jnp.concatenate([sin, sin], axis=1), LANE))


def _trunk(x3, pos0, cache_ckv, cache_kr, state_pool, state_conv, state_ffn, layers, vecs, dims):
    batch, seq, d = x3.shape
    _, pw, cw, ql, kl, rope, nope, nh = dims
    depth = len(layers)
    m = batch * seq
    x = x3.reshape(m, d)
    cached = cache_ckv is not None
    scale = 1.0 / math.sqrt(nope + rope)
    if not cached:
        scale *= math.log2(math.e)
    cos_t, sin_t = _rope_tables(pos0, seq, rope)
    row = lambda v: v.reshape(1, -1)
    h = _norm_cast(x, row(vecs["g_pre_mix"][0]))
    outs = {k: [] for k in ("ckv", "kr", "pool", "conv", "ffn")}
    for l in range(depth):
        w = layers[l]
        f = w["wdown"].shape[0]
        pa = _matmul(h, w["wa"])
        res = _mla_prep(h, w["wb"], row(vecs["g_q_a"][l]), row(vecs["g_kv_a"][l]), cos_t, sin_t,
                        w["wq"], w["wuk"], w["wuv"], seq=seq, n_heads=nh, nope=nope, rope=rope,
                        scale=scale, expand_kv=not cached)
        if cached:
            ckv, kr, q = res
            o = _cached_attn(q, cache_ckv, cache_kr, ckv, kr, w["wuk_h"], w["wuv_h"], layer=l,
                             batch=batch, seq=seq, n_heads=nh, nope=nope, rope=rope)
            hist_pool = _pad_front(state_pool[l], POOL_HIST_PAD)
            hist_conv = _pad_front(state_conv[l], CONV_HIST_PAD)
            hist_ffn = _interleave_ffn_cols(_pad_front(state_ffn[l], FFN_HIST_PAD), f)
        else:
            ckv, kr, q, kfull, vfull = res
            o = _flash(q, kfull, vfull, batch=batch, seq=seq, n_heads=nh, nope=nope)
            hist_pool = jnp.zeros((batch, POOL_HIST_PAD, pw), F32)
            hist_conv = jnp.zeros((batch, CONV_HIST_PAD, cw), F32)
            hist_ffn = jnp.zeros((batch, FFN_HIST_PAD, w["wfdw"].shape[1]), F32)
        x, h2, st_pool, st_conv = _mixer(
            pa, o, x, hist_pool, hist_conv, row(vecs["b_gate"][l]), w["wpool"],
            row(vecs["pool_scale"][l]), w["wdw"], row(vecs["b_conv_dw"][l]),
            row(vecs["g_conv_ln"][l]), row(vecs["b_conv_ln"][l]), w["wpw"], w["wo"], w["wout"],
            row(vecs["g_post_mix"][l]), row(vecs["g_pre_ffn"][l]), batch=batch, seq=seq, pos0=pos0)
        g_next = row(vecs["g_pre_mix"][l + 1]) if l + 1 < depth else None
        res = _ffn(h2, hist_ffn, w["wup"], w["wfdw"],
                   row(_interleave_ffn_cols(vecs["b_ffn_dw"][l], f)), w["wdown"], x,
                   row(vecs["g_post_ffn"][l]), g_next, seq=seq)
        if g_next is None:
            x, tails = res
        else:
            x, h, tails = res
        outs["ckv"].append(ckv.reshape(batch, seq, kl))
        outs["kr"].append(kr.reshape(batch, seq, rope))
        outs["pool"].append(st_pool[:, POOL_HIST_PAD - POOL_HIST:])
        outs["conv"].append(st_conv[:, CONV_HIST_PAD - CONV_HIST:])
        tails = _deinterleave_ffn_cols(tails, f)
        tails = tails.reshape(batch, -1, FFN_HIST_PAD, 2 * f)[:, -1]
        outs["ffn"].append(tails[:, FFN_HIST_PAD - FFN_HIST:])
    return (x.reshape(batch, seq, d),) + tuple(jnp.stack(outs[k]) for k in ("ckv", "kr", "pool", "conv", "ffn"))


def kernel(x_prompt, x_sample, cache_ckv, cache_krope, state_pool, state_conv, state_ffn,
           g_pre_mix, w_in, b_gate, g_q_a, g_kv_a, w_uq, w_uk, w_uv, w_o_mla,
           w_pool, pool_scale, w_conv_dw, b_conv_dw, g_conv_ln, b_conv_ln, w_conv_pw,
           w_out, g_post_mix, g_pre_ffn, w_up, w_ffn_dw, b_ffn_dw, w_down, g_post_ffn):
    depth, d = g_pre_mix.shape
    nh = w_uq.shape[2]
    nope = w_uk.shape[3]
    rope = w_uq.shape[3] - nope
    dims = (d, state_pool.shape[3], state_conv.shape[3], w_uq.shape[1], w_uk.shape[1], rope, nope, nh)
    layers = [_prep_layer(l, dims, w_in, w_uq, w_uk, w_uv, w_o_mla, w_pool, w_conv_dw, w_conv_pw,
                          w_out, w_up, w_ffn_dw, w_down) for l in range(depth)]
    vecs = dict(g_pre_mix=g_pre_mix, b_gate=b_gate, g_q_a=g_q_a, g_kv_a=g_kv_a,
                pool_scale=pool_scale, b_conv_dw=b_conv_dw, g_conv_ln=g_conv_ln,
                b_conv_ln=b_conv_ln, g_post_mix=g_post_mix, g_pre_ffn=g_pre_ffn,
                b_ffn_dw=b_ffn_dw, g_post_ffn=g_post_ffn)
    prompt = _trunk(x_prompt, 0, None, None, None, None, None, layers, vecs, dims)
    sample = _trunk(x_sample, cache_ckv.shape[2], cache_ckv, cache_krope, state_pool, state_conv,
                    state_ffn, layers, vecs, dims)
    return (prompt[0], sample[0]) + prompt[1:] + sample[1:]
```
